```python
import jax, jax.numpy as jnp
from jax import lax
import numpy as np

D_MODEL = 2048
BATCH = 8
SEQ = 2048
DEPTH = 2

PLE_DIM = 256
MIX_WIDTH = D_MODEL
DN_WIDTH = MIX_WIDTH // 2
HG_WIDTH = MIX_WIDTH - DN_WIDTH
DN_HEAD_DIM = 128
DN_HEADS = DN_WIDTH // DN_HEAD_DIM
HG_HEAD_DIM = 128
HG_HEADS = HG_WIDTH // HG_HEAD_DIM
CONV_WIDTH = 4
DN_CHUNK = 64
HG_CHUNK = 64
NORM_EPS = 1e-6
L2_EPS = 1e-6
SPLITS = (3 * DN_WIDTH, DN_WIDTH, DN_HEADS, DN_HEADS, HG_WIDTH, HG_WIDTH, HG_WIDTH, HG_WIDTH)
IN_WIDTH = sum(SPLITS)

kernel_name = "hybrid_gdn_hgrn2_parallel_heads"


def rms_norm(x, w):
    xf = x.astype(jnp.float32)
    y = xf * lax.rsqrt(jnp.mean(xf * xf, axis=-1, keepdims=True) + NORM_EPS)
    return (y * w.astype(jnp.float32)).astype(x.dtype)


def l2_norm(x):
    return x * lax.rsqrt(jnp.sum(x * x, axis=-1, keepdims=True) + L2_EPS)


def masked_exp(mask, diff):
    return jnp.where(mask, jnp.exp(jnp.where(mask, diff, 0.0)), 0.0)


def to_heads(t, n_heads):
    B, S, W = t.shape
    return t.reshape(B, S, n_heads, W // n_heads).transpose(0, 2, 1, 3)


def gated_head_norm(o, z, w):
    B, H, S, d = o.shape
    o = o.transpose(0, 2, 1, 3)
    o = o * lax.rsqrt(jnp.mean(o * o, axis=-1, keepdims=True) + NORM_EPS) * w.astype(jnp.float32)
    o = o * jax.nn.silu(z.astype(jnp.float32).reshape(B, S, H, d))
    return o.reshape(B, S, H * d)


def causal_conv(x, w):
    C = x.shape[-1]
    return lax.conv_general_dilated(
        x, w[:, None, :].astype(x.dtype), window_strides=(1,),
        padding=[(CONV_WIDTH - 1, 0)], dimension_numbers=("NWC", "WIO", "NWC"),
        feature_group_count=C)


def chunk_gated_delta_rule(q, k, v, g, beta):
    B, H, S, dk = q.shape
    dv = v.shape[-1]
    C = DN_CHUNK
    N = S // C
    q, k, v = (t.reshape(B, H, N, C, t.shape[-1]) for t in (q, k, v))
    g = g.reshape(B, H, N, C)
    beta = beta.reshape(B, H, N, C)
    G = jnp.cumsum(g, axis=-1)
    causal = jnp.tril(jnp.ones((C, C), bool))
    strict = jnp.tril(jnp.ones((C, C), bool), -1)
    decay = masked_exp(causal, G[..., :, None] - G[..., None, :])
    k_beta = k * beta[..., None]
    A = jnp.where(strict, jnp.einsum("bhncd,bhnsd->bhncs", k_beta, k) * decay, 0.0)
    eye = jnp.eye(C, dtype=A.dtype)
    rhs = jnp.concatenate([v * beta[..., None], k_beta * jnp.exp(G)[..., None]], axis=-1)
    sol = lax.linalg.triangular_solve(A + eye, rhs, left_side=True, lower=True,
                                      unit_diagonal=True)
    u, w = sol[..., :dv], sol[..., dv:]
    qk = jnp.einsum("bhncd,bhnsd->bhncs", q, k) * decay
    q_decay = q * jnp.exp(G)[..., None]
    k_tail = k * jnp.exp(G[..., -1:] - G)[..., None]
    tail = jnp.exp(G[..., -1])

    def step(state, xs):
        u_c, w_c, qk_c, qd_c, kt_c, tail_c = xs
        v_new = u_c - jnp.einsum("bhcd,bhde->bhce", w_c, state)
        o = (jnp.einsum("bhcd,bhde->bhce", qd_c, state)
             + jnp.einsum("bhcs,bhse->bhce", qk_c, v_new))
        state = tail_c[..., None, None] * state + jnp.einsum("bhcd,bhce->bhde", kt_c, v_new)
        return state, o

    xs = tuple(jnp.moveaxis(t, 2, 0) for t in (u, w, qk, q_decay, k_tail, tail))
    _, o = lax.scan(step, jnp.zeros((B, H, dk, dv), q.dtype), xs)
    return jnp.moveaxis(o, 0, 2).reshape(B, H, S, dv)


def chunk_hgrn2(q, k, v, log_f):
    B, H, S, dk = q.shape
    dv = v.shape[-1]
    C = HG_CHUNK
    N = S // C
    q, k, v, log_f = (t.reshape(B, H, N, C, t.shape[-1]) for t in (q, k, v, log_f))
    G = jnp.cumsum(log_f, axis=3)
    q_decay = q * jnp.exp(G)
    k_tail = k * jnp.exp(G[:, :, :, -1:] - G)
    tail = jnp.exp(G[:, :, :, -1])
    causal = jnp.tril(jnp.ones((C, C), bool))[:, :, None]

    def step(state, xs):
        q_c, k_c, v_c, G_c, qd_c, kt_c, tail_c = xs
        rel = masked_exp(causal, G_c[:, :, :, None, :] - G_c[:, :, None, :, :])
        A = jnp.einsum("bhrd,bhsd,bhrsd->bhrs", q_c, k_c, rel)
        o = (jnp.einsum("bhrd,bhde->bhre", qd_c, state)
             + jnp.einsum("bhrs,bhse->bhre", A, v_c))
        state = tail_c[..., None] * state + jnp.einsum("bhsd,bhse->bhde", kt_c, v_c)
        return state, o

    xs = tuple(jnp.moveaxis(t, 2, 0) for t in (q, k, v, G, q_decay, k_tail, tail))
    _, o = lax.scan(step, jnp.zeros((B, H, dk, dv), q.dtype), xs)
    return jnp.moveaxis(o, 0, 2).reshape(B, H, S, dv)


def deltanet_branch(qkv, z, b, a, conv_w, A_log, dt_bias, norm_w):
    f32 = jnp.float32
    qkv = jax.nn.silu(causal_conv(qkv.astype(f32), conv_w.astype(f32)))
    q, k, v = jnp.split(qkv, 3, axis=-1)
    q = l2_norm(to_heads(q, DN_HEADS)) * (DN_HEAD_DIM ** -0.5)
    k = l2_norm(to_heads(k, DN_HEADS))
    v = to_heads(v, DN_HEADS)
    beta = jax.nn.sigmoid(b.astype(f32)).transpose(0, 2, 1)
    g = -(jnp.exp(A_log.astype(f32))
          * jax.nn.softplus(a.astype(f32) + dt_bias.astype(f32))).transpose(0, 2, 1)
    o = chunk_gated_delta_rule(q, k, v, g, beta)
    return gated_head_norm(o, z, norm_w)


def hgrn2_branch(q, f, i, z, lb, norm_w):
    f32 = jnp.float32
    q = to_heads(jax.nn.silu(q.astype(f32)), HG_HEADS)
    fh = to_heads(f.astype(f32), HG_HEADS)
    lbh = lb.astype(f32).reshape(HG_HEADS, 1, HG_HEAD_DIM)
    log_f = jnp.log(lbh + (1.0 - lbh) * jax.nn.sigmoid(fh))
    k = (1.0 - lbh) * jax.nn.sigmoid(-fh)
    v = to_heads(i.astype(f32), HG_HEADS)
    o = chunk_hgrn2(q, k, v, log_f)
    return gated_head_norm(o, z, norm_w)


def setup_inputs(seed: int = 0) -> dict:
    key = jax.random.key(seed)
    ks = jax.random.split(key, 16)
    f32 = jnp.float32
    x = jax.random.normal(ks[0], (BATCH, SEQ, D_MODEL), f32)
    p = jax.random.normal(ks[1], (DEPTH, BATCH, SEQ, PLE_DIM), f32)
    norm_w = 1.0 + 0.02 * jax.random.normal(ks[2], (DEPTH, D_MODEL), f32)
    w_in = jax.random.normal(ks[3], (DEPTH, D_MODEL, IN_WIDTH), f32) * D_MODEL ** -0.5
    dn_conv_w = jax.random.normal(ks[4], (DEPTH, CONV_WIDTH, 3 * DN_WIDTH), f32) * CONV_WIDTH ** -0.5
    dn_A_log = jnp.log(jax.random.uniform(ks[5], (DEPTH, DN_HEADS), f32, 1.0, 16.0))
    dt = jnp.exp(jax.random.uniform(ks[6], (DEPTH, DN_HEADS), f32, np.log(1e-3), np.log(1e-1)))
    dn_dt_bias = dt + jnp.log(-jnp.expm1(-dt))
    dn_norm_w = 1.0 + 0.02 * jax.random.normal(ks[7], (DEPTH, DN_HEAD_DIM), f32)
    hg_lb_logits = 0.5 * jax.random.normal(ks[8], (DEPTH, HG_WIDTH), f32)
    hg_norm_w = 1.0 + 0.02 * jax.random.normal(ks[9], (DEPTH, HG_HEAD_DIM), f32)
    w_out = jax.random.normal(ks[10], (DEPTH, MIX_WIDTH, D_MODEL), f32) * MIX_WIDTH ** -0.5
    w_ple_up = jax.random.normal(ks[11], (DEPTH, PLE_DIM, D_MODEL), f32) * PLE_DIM ** -0.5
    w_ple_gate = jax.random.normal(ks[12], (DEPTH, D_MODEL, D_MODEL), f32) * D_MODEL ** -0.5
    final_norm_w = 1.0 + 0.02 * jax.random.normal(ks[13], (D_MODEL,), f32)
    return {"x": x, "p": p, "norm_w": norm_w, "w_in": w_in, "dn_conv_w": dn_conv_w,
            "dn_A_log": dn_A_log, "dn_dt_bias": dn_dt_bias, "dn_norm_w": dn_norm_w,
            "hg_lb_logits": hg_lb_logits, "hg_norm_w": hg_norm_w, "w_out": w_out,
            "w_ple_up": w_ple_up, "w_ple_gate": w_ple_gate, "final_norm_w": final_norm_w}


def reference(x, p, norm_w, w_in, dn_conv_w, dn_A_log, dn_dt_bias, dn_norm_w,
              hg_lb_logits, hg_norm_w, w_out, w_ple_up, w_ple_gate, final_norm_w):
    split_points = [int(s) for s in np.cumsum(SPLITS)[:-1]]
    lb_probs = jax.nn.softmax(hg_lb_logits.astype(jnp.float32), axis=0)
    lower_bounds = jnp.cumsum(lb_probs, axis=0) - lb_probs[0]
    h = x
    for l in range(DEPTH):
        hn = rms_norm(h, norm_w[l])
        proj = hn @ w_in[l]
        dn_qkv, dn_z, dn_b, dn_a, hg_q, hg_f, hg_i, hg_z = jnp.split(proj, split_points, axis=-1)
        y_a = deltanet_branch(dn_qkv, dn_z, dn_b, dn_a, dn_conv_w[l], dn_A_log[l],
                              dn_dt_bias[l], dn_norm_w[l])
        y_b = hgrn2_branch(hg_q, hg_f, hg_i, hg_z, lower_bounds[l], hg_norm_w[l])
        y = jnp.concatenate([y_a, y_b], axis=-1).astype(h.dtype)
        h = h + y @ w_out[l]
        gate = jax.nn.sigmoid((h @ w_ple_gate[l]).astype(jnp.float32))
        h = h + ((p[l] @ w_ple_up[l]).astype(jnp.float32) * gate).astype(h.dtype)
    return rms_norm(h, final_norm_w)
```

```python
import functools

import jax
import jax.numpy as jnp
import numpy as np
from jax import lax
from jax.experimental import pallas as pl
from jax.experimental.pallas import tpu as pltpu

F32 = jnp.float32
BF16 = jnp.bfloat16

HEAD_DIM = 128
CHUNK = 64
CONV_WIDTH = 4
NORM_EPS = 1e-6
L2_EPS = 1e-6
N_LEVELS = 6
VMEM_LIMIT_BYTES = 56 * 1024 * 1024

IN_ROWS = 1024
IN_COLS = 512
OUT_ROWS = 256
REC_CHUNKS = 4
REC_ROWS = REC_CHUNKS * CHUNK


def _bdot(a, b):
    return jnp.dot(a.astype(BF16), b.astype(BF16), preferred_element_type=F32)


def _bdot_nt(a, b):
    return lax.dot_general(a.astype(BF16), b.astype(BF16), (((1,), (1,)), ((), ())),
                           preferred_element_type=F32)


def _bdot_tn(a, b):
    return lax.dot_general(a.astype(BF16), b.astype(BF16), (((0,), (0,)), ((), ())),
                           preferred_element_type=F32)


def _split2(x):
    hi = x.astype(BF16)
    lo = (x - hi.astype(F32)).astype(BF16)
    return hi, lo


def _dot_exact_rhs(a, b01):
    hi, lo = _split2(a)
    b = b01.astype(BF16)
    return (jnp.dot(hi, b, preferred_element_type=F32)
            + jnp.dot(lo, b, preferred_element_type=F32))


def _dot3(a, b):
    ah, al = _split2(a)
    bh, bl = _split2(b)
    return (jnp.dot(ah, bh, preferred_element_type=F32)
            + jnp.dot(ah, bl, preferred_element_type=F32)
            + jnp.dot(al, bh, preferred_element_type=F32))


def _sigmoid_pair(x):
    e = jnp.exp(-jnp.abs(x))
    r = 1.0 / (1.0 + e)
    er = e * r
    pos = x >= 0
    return jnp.where(pos, r, er), jnp.where(pos, er, r)


def _silu(x):
    return x * _sigmoid_pair(x)[0]


def _softplus(x):
    return jnp.maximum(x, 0.0) + jnp.log1p(jnp.exp(-jnp.abs(x)))


def _gated_head_norm(o, z, w):
    o = o * lax.rsqrt(jnp.mean(o * o, axis=-1, keepdims=True) + NORM_EPS) * w
    return o * _silu(z)


def _inproj_kernel(h_ref, nw_ref, w_ref, wbat_ref, proj_ref, bat_ref, hn_ref):
    @pl.when(pl.program_id(1) == 0)
    def _():
        x = h_ref[...]
        ms = jnp.mean(x * x, axis=-1, keepdims=True)
        hn = (x * lax.rsqrt(ms + NORM_EPS) * nw_ref[...]).astype(BF16)
        hn_ref[...] = hn
        bat_ref[...] = lax.dot_general(wbat_ref[...], hn, (((1,), (1,)), ((), ())),
                                       preferred_element_type=F32)

    proj_ref[...] = jnp.dot(hn_ref[...], w_ref[...], preferred_element_type=F32)


def _inproj(h, nw, w_main, w_bat):
    tokens, d_model = h.shape
    width = w_main.shape[1]
    n_gate = w_bat.shape[0]
    grid = (tokens // IN_ROWS, width // IN_COLS)
    return pl.pallas_call(
        _inproj_kernel,
        grid=grid,
        in_specs=[
            pl.BlockSpec((IN_ROWS, d_model), lambda i, j: (i, 0)),
            pl.BlockSpec((1, d_model), lambda i, j: (0, 0)),
            pl.BlockSpec((d_model, IN_COLS), lambda i, j: (0, j)),
            pl.BlockSpec((n_gate, d_model), lambda i, j: (0, 0)),
        ],
        out_specs=[
            pl.BlockSpec((IN_ROWS, IN_COLS), lambda i, j: (i, j)),
            pl.BlockSpec((n_gate, IN_ROWS), lambda i, j: (0, i)),
        ],
        out_shape=[
            jax.ShapeDtypeStruct((tokens, width), F32),
            jax.ShapeDtypeStruct((n_gate, tokens), F32),
        ],
        scratch_shapes=[pltpu.VMEM((IN_ROWS, d_model), BF16)],
        compiler_params=pltpu.CompilerParams(
            dimension_semantics=("arbitrary", "arbitrary"),
            vmem_limit_bytes=VMEM_LIMIT_BYTES),
        name="inproj",
    )(h, nw, w_main, w_bat)


def _iota2(shape, dim):
    return lax.broadcasted_iota(jnp.int32, shape, dim)


def _conv_silu(x_ref, w_ref, buf_ref):
    rows = x_ref.shape[0]
    x = x_ref[...]
    buf_ref[pl.ds(8, rows), :] = x
    w = w_ref[...]
    acc = w[CONV_WIDTH - 1:CONV_WIDTH, :] * x
    for j in range(CONV_WIDTH - 1):
        shift = CONV_WIDTH - 1 - j
        acc = acc + w[j:j + 1, :] * buf_ref[pl.ds(8 - shift, rows), :]
    buf_ref[pl.ds(0, 8), :] = x[rows - 8:rows, :]
    return _silu(acc)


def _deltanet_kernel(alog_ref, dtb_ref, q_ref, k_ref, v_ref, z_ref, b_ref, a_ref,
                     wq_ref, wk_ref, wv_ref, nw_ref, y_ref,
                     state_ref, qbuf_ref, kbuf_ref, vbuf_ref):
    head = pl.program_id(1)

    @pl.when(pl.program_id(2) == 0)
    def _():
        state_ref[...] = jnp.zeros_like(state_ref)
        qbuf_ref[pl.ds(0, 8), :] = jnp.zeros((8, HEAD_DIM), F32)
        kbuf_ref[pl.ds(0, 8), :] = jnp.zeros((8, HEAD_DIM), F32)
        vbuf_ref[pl.ds(0, 8), :] = jnp.zeros((8, HEAD_DIM), F32)

    q = _conv_silu(q_ref, wq_ref, qbuf_ref)
    k = _conv_silu(k_ref, wk_ref, kbuf_ref)
    v = _conv_silu(v_ref, wv_ref, vbuf_ref)
    q = q * lax.rsqrt(jnp.sum(q * q, axis=-1, keepdims=True) + L2_EPS) * (HEAD_DIM ** -0.5)
    k = k * lax.rsqrt(jnp.sum(k * k, axis=-1, keepdims=True) + L2_EPS)

    rows = q.shape[0]
    beta_row = _sigmoid_pair(b_ref[...])[0]
    rate = jnp.exp(jnp.full((1, rows), alog_ref[head], F32))
    g_row = -rate * _softplus(a_ref[...] + dtb_ref[head])

    r_i = _iota2((CHUNK, CHUNK), 0)
    c_i = _iota2((CHUNK, CHUNK), 1)
    causal = c_i <= r_i
    strict = c_i < r_i
    eye = c_i == r_i
    upper01 = (r_i <= c_i).astype(F32)
    ones_rhs = jnp.ones((CHUNK, HEAD_DIM), BF16)
    eye_f = eye.astype(F32)

    state = state_ref[...]
    outs = []
    for c in range(rows // CHUNK):
        sl = slice(c * CHUNK, (c + 1) * CHUNK)
        qc, kc, vc = q[sl], k[sl], v[sl]
        g_b = jnp.broadcast_to(g_row[:, sl], (CHUNK, CHUNK))
        beta_b = jnp.broadcast_to(beta_row[:, sl], (CHUNK, CHUNK))
        lhs = jnp.concatenate([jnp.where(causal, g_b, 0.0), jnp.where(eye, beta_b, 0.0), g_b], axis=0)
        cols = _dot_exact_rhs(lhs, ones_rhs)
        g_col = cols[0:CHUNK]
        beta_col = cols[CHUNK:2 * CHUNK]
        g_last = cols[2 * CHUNK:3 * CHUNK]
        g_rowcum = _dot_exact_rhs(g_b, upper01)
        diff = g_col[:, 0:CHUNK] - g_rowcum
        decay = jnp.where(causal, jnp.exp(jnp.where(causal, diff, 0.0)), 0.0)

        exp_g = jnp.exp(g_col)
        k_beta = kc * beta_col
        prods = _bdot_nt(jnp.concatenate([k_beta, qc], axis=0), kc)
        a_mat = jnp.where(strict, prods[0:CHUNK] * decay, 0.0)
        qk = prods[CHUNK:2 * CHUNK] * decay

        n_pow = -a_mat
        inv = eye_f + n_pow
        n_pow = _dot3(n_pow, n_pow)
        for _ in range(N_LEVELS - 2):
            both = _dot3(jnp.concatenate([inv, n_pow], axis=0), n_pow)
            inv = inv + both[0:CHUNK]
            n_pow = both[CHUNK:2 * CHUNK]
        inv = inv + _dot3(inv, n_pow)

        rhs = jnp.concatenate([vc * beta_col, k_beta * exp_g], axis=1)
        sol = _bdot(inv, rhs)
        u = sol[:, 0:HEAD_DIM]
        w = sol[:, HEAD_DIM:2 * HEAD_DIM]

        q_decay = qc * exp_g
        k_tail = kc * jnp.exp(g_last - g_col)
        tail = jnp.exp(g_last[0:1, :])

        ws_qs = _bdot(jnp.concatenate([w, q_decay], axis=0), state)
        v_new = u - ws_qs[0:CHUNK]
        outs.append(ws_qs[CHUNK:2 * CHUNK] + _bdot(qk, v_new))
        state = tail * state + _bdot_tn(k_tail, v_new)

    state_ref[...] = state
    o = jnp.concatenate(outs, axis=0)
    y_ref[...] = _gated_head_norm(o, z_ref[...], nw_ref[...]).astype(y_ref.dtype)


def _deltanet(proj, bat3, conv_w, a_log, dt_bias, norm_w, batch, seq, n_heads):
    tokens = proj.shape[0]
    steps = seq // REC_ROWS
    row_map = lambda col0: (lambda b, h, t: (b * steps + t, col0 + h))
    gate_map = lambda row0: (lambda b, h, t: (row0 + h, 0, b * steps + t))
    conv_map = lambda col0: (lambda b, h, t: (0, col0 + h))
    smem = pl.BlockSpec(memory_space=pltpu.SMEM)
    return pl.pallas_call(
        _deltanet_kernel,
        grid=(batch, n_heads, steps),
        in_specs=[
            smem, smem,
            pl.BlockSpec((REC_ROWS, HEAD_DIM), row_map(0)),
            pl.BlockSpec((REC_ROWS, HEAD_DIM), row_map(n_heads)),
            pl.BlockSpec((REC_ROWS, HEAD_DIM), row_map(2 * n_heads)),
            pl.BlockSpec((REC_ROWS, HEAD_DIM), row_map(3 * n_heads)),
            pl.BlockSpec((None, 1, REC_ROWS), gate_map(0)),
            pl.BlockSpec((None, 1, REC_ROWS), gate_map(n_heads)),
            pl.BlockSpec((CONV_WIDTH, HEAD_DIM), conv_map(0)),
            pl.BlockSpec((CONV_WIDTH, HEAD_DIM), conv_map(n_heads)),
            pl.BlockSpec((CONV_WIDTH, HEAD_DIM), conv_map(2 * n_heads)),
            pl.BlockSpec((1, HEAD_DIM), lambda b, h, t: (0, 0)),
        ],
        out_specs=pl.BlockSpec((REC_ROWS, HEAD_DIM), lambda b, h, t: (b * steps + t, h)),
        out_shape=jax.ShapeDtypeStruct((tokens, n_heads * HEAD_DIM), BF16),
        scratch_shapes=[
            pltpu.VMEM((HEAD_DIM, HEAD_DIM), F32),
            pltpu.VMEM((REC_ROWS + 8, HEAD_DIM), F32),
            pltpu.VMEM((REC_ROWS + 8, HEAD_DIM), F32),
            pltpu.VMEM((REC_ROWS + 8, HEAD_DIM), F32),
        ],
        compiler_params=pltpu.CompilerParams(
            dimension_semantics=("arbitrary", "arbitrary", "arbitrary"),
            vmem_limit_bytes=VMEM_LIMIT_BYTES),
        name="deltanet",
    )(a_log, dt_bias, proj, proj, proj, proj, bat3, bat3, conv_w, conv_w, conv_w, norm_w)


def _level_constants():
    t = np.arange(CHUNK)
    sums = [(t[None, :] <= t[:, None])]
    masks = [np.eye(CHUNK, dtype=bool)]
    for lvl in range(1, N_LEVELS + 1):
        size, half = 1 << lvl, 1 << (lvl - 1)
        ref = (t // size) * size + half - 1
        sums.append(t[None, :] <= ref[:, None])
        same = (t[:, None] // size) == (t[None, :] // size)
        upper_r = ((t // half) % 2 == 1)[:, None]
        lower_s = ((t // half) % 2 == 0)[None, :]
        masks.append(same & upper_r & lower_s)
    sums.append(np.ones((CHUNK, CHUNK), dtype=bool))
    sums = np.concatenate(sums, axis=0).astype(np.float32)
    sums = np.concatenate([sums, sums], axis=1)
    masks = np.stack(masks, axis=0).astype(np.float32)
    assert masks.sum(axis=0).tolist() == np.tril(np.ones((CHUNK, CHUNK))).tolist()
    return sums, masks


def _hgrn2_kernel(layer, q_ref, f_ref, i_ref, z_ref, lbl_ref, nw_ref, sums_ref, masks_ref, y_ref,
                  state_ref):
    @pl.when(pl.program_id(2) == 0)
    def _():
        state_ref[...] = jnp.zeros_like(state_ref)

    logits = lbl_ref[...]
    ex = jnp.exp(logits - jnp.max(logits, axis=0, keepdims=True))
    probs = ex / jnp.sum(ex, axis=0, keepdims=True)
    lb = jnp.sum(probs[0:layer + 1], axis=0, keepdims=True) - probs[0:1]

    sig, nsig = _sigmoid_pair(f_ref[...])
    log_f = jnp.log(lb + (1.0 - lb) * sig)
    k = (1.0 - lb) * nsig
    q = _silu(q_ref[...])
    v = i_ref[...]

    sums2 = sums_ref[...]

    state_t = state_ref[...]
    rows = q.shape[0]
    outs = []
    for c in range(rows // CHUNK):
        sl = slice(c * CHUNK, (c + 1) * CHUNK)
        qc, kc, vc = q[sl], k[sl], v[sl]
        hi, lo = _split2(log_f[sl])
        g_all = jnp.dot(sums2, jnp.concatenate([hi, lo], axis=0), preferred_element_type=F32)
        g = g_all[0:CHUNK]
        g_last = g_all[(N_LEVELS + 1) * CHUNK:(N_LEVELS + 2) * CHUNK]

        a_mat = masks_ref[0] * _bdot_nt(qc, kc)
        for lvl in range(1, N_LEVELS + 1):
            g_ref = g_all[lvl * CHUNK:(lvl + 1) * CHUNK]
            wgt = jnp.exp(-jnp.abs(g - g_ref))
            a_mat = a_mat + masks_ref[lvl] * _bdot_nt(qc * wgt, kc * wgt)

        q_decay = qc * jnp.exp(g)
        k_tail = kc * jnp.exp(g_last - g)
        tail = jnp.exp(g_last[0:1, :])

        outs.append(_bdot_nt(q_decay, state_t) + _bdot(a_mat, vc))
        state_t = tail * state_t + _bdot_tn(vc, k_tail)

    state_ref[...] = state_t
    o = jnp.concatenate(outs, axis=0)
    y_ref[...] = _gated_head_norm(o, z_ref[...], nw_ref[...]).astype(y_ref.dtype)


def _hgrn2(proj, lb_logits, norm_w, sums, masks, layer, col0, batch, seq, n_heads):
    tokens = proj.shape[0]
    depth = lb_logits.shape[0]
    steps = seq // REC_ROWS
    row_map = lambda c0: (lambda b, h, t: (b * steps + t, c0 + h))
    return pl.pallas_call(
        functools.partial(_hgrn2_kernel, layer),
        grid=(batch, n_heads, steps),
        in_specs=[
            pl.BlockSpec((REC_ROWS, HEAD_DIM), row_map(col0)),
            pl.BlockSpec((REC_ROWS, HEAD_DIM), row_map(col0 + n_heads)),
            pl.BlockSpec((REC_ROWS, HEAD_DIM), row_map(col0 + 2 * n_heads)),
            pl.BlockSpec((REC_ROWS, HEAD_DIM), row_map(col0 + 3 * n_heads)),
            pl.BlockSpec((depth, HEAD_DIM), lambda b, h, t: (0, h)),
            pl.BlockSpec((1, HEAD_DIM), lambda b, h, t: (0, 0)),
            pl.BlockSpec(sums.shape, lambda b, h, t: (0, 0)),
            pl.BlockSpec(masks.shape, lambda b, h, t: (0, 0, 0)),
        ],
        out_specs=pl.BlockSpec((REC_ROWS, HEAD_DIM), lambda b, h, t: (b * steps + t, h)),
        out_shape=jax.ShapeDtypeStruct((tokens, n_heads * HEAD_DIM), BF16),
        scratch_shapes=[pltpu.VMEM((HEAD_DIM, HEAD_DIM), F32)],
        compiler_params=pltpu.CompilerParams(
            dimension_semantics=("arbitrary", "arbitrary", "arbitrary"),
            vmem_limit_bytes=VMEM_LIMIT_BYTES),
        name="hgrn2",
    )(proj, proj, proj, proj, lb_logits, norm_w, sums, masks)


def _outproj_kernel(final, h_ref, ya_ref, yb_ref, p_ref, wo_ref, wg_ref, wu_ref, nw_ref, o_ref):
    y = jnp.concatenate([ya_ref[...], yb_ref[...]], axis=1)
    h1 = h_ref[...] + jnp.dot(y, wo_ref[...], preferred_element_type=F32)
    gate = _sigmoid_pair(jnp.dot(h1.astype(BF16), wg_ref[...], preferred_element_type=F32))[0]
    up = jnp.dot(p_ref[...].astype(BF16), wu_ref[...], preferred_element_type=F32)
    h2 = h1 + up * gate
    if final:
        ms = jnp.mean(h2 * h2, axis=-1, keepdims=True)
        h2 = h2 * lax.rsqrt(ms + NORM_EPS) * nw_ref[...]
    o_ref[...] = h2


def _outproj(h, ya, yb, p, w_out, w_gate, w_up, final_nw, final):
    tokens, d_model = h.shape
    half = ya.shape[1]
    ple = p.shape[1]
    resident = lambda shape: pl.BlockSpec(shape, lambda i: (0, 0), pipeline_mode=pl.Buffered(1))
    return pl.pallas_call(
        functools.partial(_outproj_kernel, final),
        grid=(tokens // OUT_ROWS,),
        in_specs=[
            pl.BlockSpec((OUT_ROWS, d_model), lambda i: (i, 0)),
            pl.BlockSpec((OUT_ROWS, half), lambda i: (i, 0)),
            pl.BlockSpec((OUT_ROWS, half), lambda i: (i, 0)),
            pl.BlockSpec((OUT_ROWS, ple), lambda i: (i, 0)),
            resident(w_out.shape),
            resident(w_gate.shape),
            resident(w_up.shape),
            resident(final_nw.shape),
        ],
        out_specs=pl.BlockSpec((OUT_ROWS, d_model), lambda i: (i, 0)),
        out_shape=jax.ShapeDtypeStruct((tokens, d_model), F32),
        compiler_params=pltpu.CompilerParams(
            dimension_semantics=("arbitrary",),
            vmem_limit_bytes=VMEM_LIMIT_BYTES),
        name="outproj",
    )(h, ya, yb, p, w_out, w_gate, w_up, final_nw)


def kernel(x, p, norm_w, w_in, dn_conv_w, dn_A_log, dn_dt_bias, dn_norm_w, hg_lb_logits, hg_norm_w,
           w_out, w_ple_up, w_ple_gate, final_norm_w):
    batch, seq, d_model = x.shape
    depth = w_in.shape[0]
    dn_heads = dn_A_log.shape[1]
    dn_width = dn_heads * HEAD_DIM
    hg_width = hg_lb_logits.shape[1]
    hg_heads = hg_width // HEAD_DIM
    tokens = batch * seq
    gate0 = 4 * dn_width
    gate1 = gate0 + 2 * dn_heads
    hg_col0 = gate0 // HEAD_DIM

    sums_np, masks_np = _level_constants()
    sums = jnp.asarray(sums_np, dtype=BF16)
    masks = jnp.asarray(masks_np)

    h = x.reshape(tokens, d_model)
    for l in range(depth):
        w_l = w_in[l]
        w_main = jnp.concatenate([w_l[:, :gate0], w_l[:, gate1:]], axis=1).astype(BF16)
        w_bat = w_l[:, gate0:gate1].T.astype(BF16)
        proj, bat = _inproj(h, norm_w[l][None, :], w_main, w_bat)
        bat3 = bat.reshape(2 * dn_heads, 1, tokens)
        ya = _deltanet(proj, bat3, dn_conv_w[l], dn_A_log[l], dn_dt_bias[l], dn_norm_w[l][None, :],
                       batch, seq, dn_heads)
        yb = _hgrn2(proj, hg_lb_logits, hg_norm_w[l][None, :], sums, masks, l, hg_col0,
                    batch, seq, hg_heads)
        h = _outproj(h, ya, yb, p[l].reshape(tokens, -1), w_out[l].astype(BF16),
                     w_ple_gate[l].astype(BF16), w_ple_up[l].astype(BF16), final_norm_w[None, :],
                     l == depth - 1)
    return h.reshape(batch, seq, d_model)
```

```python
import functools

import jax
import jax.numpy as jnp
import numpy as np
from jax import lax
from jax.experimental import pallas as pl
from jax.experimental.pallas import tpu as pltpu

F32 = jnp.float32
BF16 = jnp.bfloat16

HEAD_DIM = 128
CHUNK = 64
CONV_WIDTH = 4
NORM_EPS = 1e-6
L2_EPS = 1e-6
N_LEVELS = 6
VMEM_LIMIT_BYTES = 56 * 1024 * 1024

IN_ROWS = 1024
IN_COLS = 512
OUT_ROWS = 256
REC_CHUNKS = 8
REC_ROWS = REC_CHUNKS * CHUNK


def _bdot(a, b):
    return jnp.dot(a.astype(BF16), b.astype(BF16), preferred_element_type=F32)


def _bdot_nt(a, b):
    return lax.dot_general(a.astype(BF16), b.astype(BF16), (((1,), (1,)), ((), ())),
                           preferred_element_type=F32)


def _bdot_tn(a, b):
    return lax.dot_general(a.astype(BF16), b.astype(BF16), (((0,), (0,)), ((), ())),
                           preferred_element_type=F32)


def _split2(x):
    hi = x.astype(BF16)
    lo = (x - hi.astype(F32)).astype(BF16)
    return hi, lo


def _dot_exact_rhs(a, b01):
    hi, lo = _split2(a)
    b = b01.astype(BF16)
    return (jnp.dot(hi, b, preferred_element_type=F32)
            + jnp.dot(lo, b, preferred_element_type=F32))


def _dot3(a, b):
    ah, al = _split2(a)
    bh, bl = _split2(b)
    return (jnp.dot(ah, bh, preferred_element_type=F32)
            + jnp.dot(ah, bl, preferred_element_type=F32)
            + jnp.dot(al, bh, preferred_element_type=F32))


def _sigmoid_pair(x):
    e = jnp.exp(-jnp.abs(x))
    r = 1.0 / (1.0 + e)
    er = e * r
    pos = x >= 0
    return jnp.where(pos, r, er), jnp.where(pos, er, r)


def _silu(x):
    return x * _sigmoid_pair(x)[0]


def _softplus(x):
    return jnp.maximum(x, 0.0) + jnp.log1p(jnp.exp(-jnp.abs(x)))


def _gated_head_norm(o, z, w):
    o = o * lax.rsqrt(jnp.mean(o * o, axis=-1, keepdims=True) + NORM_EPS) * w
    return o * _silu(z)


def _inproj_kernel(h_ref, nw_ref, w_ref, wbat_ref, proj_ref, bat_ref, hn_ref):
    @pl.when(pl.program_id(1) == 0)
    def _():
        x = h_ref[...]
        ms = jnp.mean(x * x, axis=-1, keepdims=True)
        hn = (x * lax.rsqrt(ms + NORM_EPS) * nw_ref[...]).astype(BF16)
        hn_ref[...] = hn
        bat_ref[...] = lax.dot_general(wbat_ref[...], hn, (((1,), (1,)), ((), ())),
                                       preferred_element_type=F32)

    proj_ref[...] = jnp.dot(hn_ref[...], w_ref[...], preferred_element_type=F32)


def _inproj(h, nw, w_main, w_bat):
    tokens, d_model = h.shape
    width = w_main.shape[1]
    n_gate = w_bat.shape[0]
    grid = (tokens // IN_ROWS, width // IN_COLS)
    return pl.pallas_call(
        _inproj_kernel,
        grid=grid,
        in_specs=[
            pl.BlockSpec((IN_ROWS, d_model), lambda i, j: (i, 0)),
            pl.BlockSpec((1, d_model), lambda i, j: (0, 0)),
            pl.BlockSpec((d_model, IN_COLS), lambda i, j: (0, j)),
            pl.BlockSpec((n_gate, d_model), lambda i, j: (0, 0)),
        ],
        out_specs=[
            pl.BlockSpec((IN_ROWS, IN_COLS), lambda i, j: (i, j)),
            pl.BlockSpec((n_gate, IN_ROWS), lambda i, j: (0, i)),
        ],
        out_shape=[
            jax.ShapeDtypeStruct((tokens, width), F32),
            jax.ShapeDtypeStruct((n_gate, tokens), F32),
        ],
        scratch_shapes=[pltpu.VMEM((IN_ROWS, d_model), BF16)],
        compiler_params=pltpu.CompilerParams(
            dimension_semantics=("arbitrary", "arbitrary"),
            vmem_limit_bytes=VMEM_LIMIT_BYTES),
        name="inproj",
    )(h, nw, w_main, w_bat)


def _iota2(shape, dim):
    return lax.broadcasted_iota(jnp.int32, shape, dim)


def _conv_silu(x_ref, w_ref, buf_ref):
    rows = x_ref.shape[0]
    x = x_ref[...]
    buf_ref[pl.ds(8, rows), :] = x
    w = w_ref[...]
    acc = w[CONV_WIDTH - 1:CONV_WIDTH, :] * x
    for j in range(CONV_WIDTH - 1):
        shift = CONV_WIDTH - 1 - j
        acc = acc + w[j:j + 1, :] * buf_ref[pl.ds(8 - shift, rows), :]
    buf_ref[pl.ds(0, 8), :] = x[rows - 8:rows, :]
    return _silu(acc)


def _deltanet_kernel(alog_ref, dtb_ref, q_ref, k_ref, v_ref, z_ref, b_ref, a_ref,
                     wq_ref, wk_ref, wv_ref, nw_ref, masks_ref, y_ref,
                     state_ref, qbuf_ref, kbuf_ref, vbuf_ref):
    head = pl.program_id(1)

    @pl.when(pl.program_id(2) == 0)
    def _():
        state_ref[...] = jnp.zeros_like(state_ref)
        qbuf_ref[pl.ds(0, 8), :] = jnp.zeros((8, HEAD_DIM), F32)
        kbuf_ref[pl.ds(0, 8), :] = jnp.zeros((8, HEAD_DIM), F32)
        vbuf_ref[pl.ds(0, 8), :] = jnp.zeros((8, HEAD_DIM), F32)

    q = _conv_silu(q_ref, wq_ref, qbuf_ref)
    k = _conv_silu(k_ref, wk_ref, kbuf_ref)
    v = _conv_silu(v_ref, wv_ref, vbuf_ref)
    q = q * lax.rsqrt(jnp.sum(q * q, axis=-1, keepdims=True) + L2_EPS) * (HEAD_DIM ** -0.5)
    k = k * lax.rsqrt(jnp.sum(k * k, axis=-1, keepdims=True) + L2_EPS)

    rows = q.shape[0]
    chunks = range(rows // CHUNK)
    sls = [slice(c * CHUNK, (c + 1) * CHUNK) for c in chunks]
    beta_row = _sigmoid_pair(b_ref[...])[0]
    rate = jnp.exp(jnp.full((1, rows), alog_ref[head], F32))
    g_row = -rate * _softplus(a_ref[...] + dtb_ref[head])

    r_i = _iota2((CHUNK, CHUNK), 0)
    c_i = _iota2((CHUNK, CHUNK), 1)
    causal = c_i <= r_i
    strict = c_i < r_i
    eye = c_i == r_i
    upper01 = (r_i <= c_i).astype(F32)
    ones_rhs = jnp.ones((CHUNK, HEAD_DIM), BF16)

    qs = [q[sl] for sl in sls]
    ks = [k[sl] for sl in sls]
    g_bs = [jnp.broadcast_to(g_row[:, sl], (CHUNK, CHUNK)) for sl in sls]
    cols = []
    for c in chunks:
        beta_b = jnp.broadcast_to(beta_row[:, sls[c]], (CHUNK, CHUNK))
        lhs = jnp.concatenate(
            [jnp.where(causal, g_bs[c], 0.0), jnp.where(eye, beta_b, 0.0), g_bs[c]], axis=0)
        cols.append(_dot_exact_rhs(lhs, ones_rhs))
    g_rowcums = [_dot_exact_rhs(g_bs[c], upper01) for c in chunks]

    g_cols = [cols[c][0:CHUNK] for c in chunks]
    beta_cols = [cols[c][CHUNK:2 * CHUNK] for c in chunks]
    g_lasts = [cols[c][2 * CHUNK:3 * CHUNK] for c in chunks]
    exp_gs = [jnp.exp(g_cols[c]) for c in chunks]
    k_betas = [ks[c] * beta_cols[c] for c in chunks]
    prods = [_bdot_nt(jnp.concatenate([k_betas[c], qs[c]], axis=0), ks[c]) for c in chunks]

    a_mats, qks, invs = [], [], []
    for c in chunks:
        diff = g_cols[c][:, 0:CHUNK] - g_rowcums[c]
        decay = jnp.where(causal, jnp.exp(jnp.where(causal, diff, 0.0)), 0.0)
        a_mat = jnp.where(strict, prods[c][0:CHUNK] * decay, 0.0)
        a_mats.append(a_mat)
        qks.append(prods[c][CHUNK:2 * CHUNK] * decay)
        invs.append(jnp.where(eye, 1.0, 0.0) - masks_ref[1] * a_mat)

    for lvl in range(2, N_LEVELS + 1):
        xs = [_dot3(masks_ref[lvl] * a_mats[c], invs[c]) for c in chunks]
        invs = [invs[c] - _dot3(invs[c], xs[c]) for c in chunks]

    sols = []
    for c in chunks:
        rhs = jnp.concatenate([v[sls[c]] * beta_cols[c], k_betas[c] * exp_gs[c]], axis=1)
        sols.append(_bdot(invs[c], rhs))
    k_tails = [ks[c] * jnp.exp(g_lasts[c] - g_cols[c]) for c in chunks]
    qk_sols = [_bdot(qks[c], sols[c]) for c in chunks]
    kt_sols = [_bdot_tn(k_tails[c], sols[c]) for c in chunks]

    state = state_ref[...]
    outs = []
    for c in chunks:
        lhs = jnp.concatenate([-kt_sols[c][:, HEAD_DIM:2 * HEAD_DIM],
                               qs[c] * exp_gs[c] - qk_sols[c][:, HEAD_DIM:2 * HEAD_DIM]], axis=0)
        both = _bdot(lhs, state)
        tail = jnp.exp(g_lasts[c][0:1, :])
        state = tail * state + both[0:HEAD_DIM] + kt_sols[c][:, 0:HEAD_DIM]
        outs.append(both[HEAD_DIM:HEAD_DIM + CHUNK] + qk_sols[c][:, 0:HEAD_DIM])

    state_ref[...] = state
    o = jnp.concatenate(outs, axis=0)
    y_ref[...] = _gated_head_norm(o, z_ref[...], nw_ref[...]).astype(y_ref.dtype)


def _deltanet(proj, bat3, conv_w, a_log, dt_bias, norm_w, masks, batch, seq, n_heads):
    tokens = proj.shape[0]
    steps = seq // REC_ROWS
    row_map = lambda col0: (lambda b, h, t: (b * steps + t, col0 + h))
    gate_map = lambda row0: (lambda b, h, t: (row0 + h, 0, b * steps + t))
    conv_map = lambda col0: (lambda b, h, t: (0, col0 + h))
    smem = pl.BlockSpec(memory_space=pltpu.SMEM)
    return pl.pallas_call(
        _deltanet_kernel,
        grid=(batch, n_heads, steps),
        in_specs=[
            smem, smem,
            pl.BlockSpec((REC_ROWS, HEAD_DIM), row_map(0)),
            pl.BlockSpec((REC_ROWS, HEAD_DIM), row_map(n_heads)),
            pl.BlockSpec((REC_ROWS, HEAD_DIM), row_map(2 * n_heads)),
            pl.BlockSpec((REC_ROWS, HEAD_DIM), row_map(3 * n_heads)),
            pl.BlockSpec((None, 1, REC_ROWS), gate_map(0)),
            pl.BlockSpec((None, 1, REC_ROWS), gate_map(n_heads)),
            pl.BlockSpec((CONV_WIDTH, HEAD_DIM), conv_map(0)),
            pl.BlockSpec((CONV_WIDTH, HEAD_DIM), conv_map(n_heads)),
            pl.BlockSpec((CONV_WIDTH, HEAD_DIM), conv_map(2 * n_heads)),
            pl.BlockSpec((1, HEAD_DIM), lambda b, h, t: (0, 0)),
            pl.BlockSpec(masks.shape, lambda b, h, t: (0, 0, 0)),
        ],
        out_specs=pl.BlockSpec((REC_ROWS, HEAD_DIM), lambda b, h, t: (b * steps + t, h)),
        out_shape=jax.ShapeDtypeStruct((tokens, n_heads * HEAD_DIM), BF16),
        scratch_shapes=[
            pltpu.VMEM((HEAD_DIM, HEAD_DIM), F32),
            pltpu.VMEM((REC_ROWS + 8, HEAD_DIM), F32),
            pltpu.VMEM((REC_ROWS + 8, HEAD_DIM), F32),
            pltpu.VMEM((REC_ROWS + 8, HEAD_DIM), F32),
        ],
        compiler_params=pltpu.CompilerParams(
            dimension_semantics=("arbitrary", "arbitrary", "arbitrary"),
            vmem_limit_bytes=VMEM_LIMIT_BYTES),
        name="deltanet",
    )(a_log, dt_bias, proj, proj, proj, proj, bat3, bat3, conv_w, conv_w, conv_w, norm_w, masks)


def _level_constants():
    t = np.arange(CHUNK)
    sums = [(t[None, :] <= t[:, None])]
    masks = [np.eye(CHUNK, dtype=bool)]
    for lvl in range(1, N_LEVELS + 1):
        size, half = 1 << lvl, 1 << (lvl - 1)
        ref = (t // size) * size + half - 1
        sums.append(t[None, :] <= ref[:, None])
        same = (t[:, None] // size) == (t[None, :] // size)
        upper_r = ((t // half) % 2 == 1)[:, None]
        lower_s = ((t // half) % 2 == 0)[None, :]
        masks.append(same & upper_r & lower_s)
    sums.append(np.ones((CHUNK, CHUNK), dtype=bool))
    sums = np.concatenate(sums, axis=0).astype(np.float32)
    sums = np.concatenate([sums, sums], axis=1)
    masks = np.stack(masks, axis=0).astype(np.float32)
    assert masks.sum(axis=0).tolist() == np.tril(np.ones((CHUNK, CHUNK))).tolist()
    return sums, masks


def _hgrn2_kernel(layer, q_ref, f_ref, i_ref, z_ref, lbl_ref, nw_ref, sums_ref, masks_ref, y_ref,
                  state_ref):
    @pl.when(pl.program_id(2) == 0)
    def _():
        state_ref[...] = jnp.zeros_like(state_ref)

    logits = lbl_ref[...]
    ex = jnp.exp(logits - jnp.max(logits, axis=0, keepdims=True))
    probs = ex / jnp.sum(ex, axis=0, keepdims=True)
    lb = jnp.sum(probs[0:layer + 1], axis=0, keepdims=True) - probs[0:1]

    sig, nsig = _sigmoid_pair(f_ref[...])
    log_f = jnp.log(lb + (1.0 - lb) * sig)
    k = (1.0 - lb) * nsig
    q = _silu(q_ref[...])
    v = i_ref[...]

    sums2 = sums_ref[...]

    rows = q.shape[0]
    chunks = range(rows // CHUNK)
    sls = [slice(c * CHUNK, (c + 1) * CHUNK) for c in chunks]

    g_alls = []
    for c in chunks:
        hi, lo = _split2(log_f[sls[c]])
        g_alls.append(jnp.dot(sums2, jnp.concatenate([hi, lo], axis=0), preferred_element_type=F32))
    gs = [g_alls[c][0:CHUNK] for c in chunks]
    g_lasts = [g_alls[c][(N_LEVELS + 1) * CHUNK:(N_LEVELS + 2) * CHUNK] for c in chunks]

    a_mats = [masks_ref[0] * _bdot_nt(q[sls[c]], k[sls[c]]) for c in chunks]
    for lvl in range(1, N_LEVELS + 1):
        for c in chunks:
            g_ref = g_alls[c][lvl * CHUNK:(lvl + 1) * CHUNK]
            wgt = jnp.exp(-jnp.abs(gs[c] - g_ref))
            a_mats[c] = a_mats[c] + masks_ref[lvl] * _bdot_nt(q[sls[c]] * wgt, k[sls[c]] * wgt)

    incs = [_bdot_tn(v[sls[c]], k[sls[c]] * jnp.exp(g_lasts[c] - gs[c])) for c in chunks]
    intras = [_bdot(a_mats[c], v[sls[c]]) for c in chunks]

    state_t = state_ref[...]
    outs = []
    for c in chunks:
        outs.append(_bdot_nt(q[sls[c]] * jnp.exp(gs[c]), state_t) + intras[c])
        state_t = jnp.exp(g_lasts[c][0:1, :]) * state_t + incs[c]

    state_ref[...] = state_t
    o = jnp.concatenate(outs, axis=0)
    y_ref[...] = _gated_head_norm(o, z_ref[...], nw_ref[...]).astype(y_ref.dtype)


def _hgrn2(proj, lb_logits, norm_w, sums, masks, layer, col0, batch, seq, n_heads):
    tokens = proj.shape[0]
    depth = lb_logits.shape[0]
    steps = seq // REC_ROWS
    row_map = lambda c0: (lambda b, h, t: (b * steps + t, c0 + h))
    return pl.pallas_call(
        functools.partial(_hgrn2_kernel, layer),
        grid=(batch, n_heads, steps),
        in_specs=[
            pl.BlockSpec((REC_ROWS, HEAD_DIM), row_map(col0)),
            pl.BlockSpec((REC_ROWS, HEAD_DIM), row_map(col0 + n_heads)),
            pl.BlockSpec((REC_ROWS, HEAD_DIM), row_map(col0 + 2 * n_heads)),
            pl.BlockSpec((REC_ROWS, HEAD_DIM), row_map(col0 + 3 * n_heads)),
            pl.BlockSpec((depth, HEAD_DIM), lambda b, h, t: (0, h)),
            pl.BlockSpec((1, HEAD_DIM), lambda b, h, t: (0, 0)),
            pl.BlockSpec(sums.shape, lambda b, h, t: (0, 0)),
            pl.BlockSpec(masks.shape, lambda b, h, t: (0, 0, 0)),
        ],
        out_specs=pl.BlockSpec((REC_ROWS, HEAD_DIM), lambda b, h, t: (b * steps + t, h)),
        out_shape=jax.ShapeDtypeStruct((tokens, n_heads * HEAD_DIM), BF16),
        scratch_shapes=[pltpu.VMEM((HEAD_DIM, HEAD_DIM), F32)],
        compiler_params=pltpu.CompilerParams(
            dimension_semantics=("arbitrary", "arbitrary", "arbitrary"),
            vmem_limit_bytes=VMEM_LIMIT_BYTES),
        name="hgrn2",
    )(proj, proj, proj, proj, lb_logits, norm_w, sums, masks)


def _outproj_kernel(final, h_ref, ya_ref, yb_ref, p_ref, wo_ref, wg_ref, wu_ref, nw_ref, o_ref):
    y = jnp.concatenate([ya_ref[...], yb_ref[...]], axis=1)
    h1 = h_ref[...] + jnp.dot(y, wo_ref[...], preferred_element_type=F32)
    gate = _sigmoid_pair(jnp.dot(h1.astype(BF16), wg_ref[...], preferred_element_type=F32))[0]
    up = jnp.dot(p_ref[...].astype(BF16), wu_ref[...], preferred_element_type=F32)
    h2 = h1 + up * gate
    if final:
        ms = jnp.mean(h2 * h2, axis=-1, keepdims=True)
        h2 = h2 * lax.rsqrt(ms + NORM_EPS) * nw_ref[...]
    o_ref[...] = h2


def _outproj(h, ya, yb, p, w_out, w_gate, w_up, final_nw, final):
    tokens, d_model = h.shape
    half = ya.shape[1]
    ple = p.shape[1]
    resident = lambda shape: pl.BlockSpec(shape, lambda i: (0, 0), pipeline_mode=pl.Buffered(1))
    return pl.pallas_call(
        functools.partial(_outproj_kernel, final),
        grid=(tokens // OUT_ROWS,),
        in_specs=[
            pl.BlockSpec((OUT_ROWS, d_model), lambda i: (i, 0)),
            pl.BlockSpec((OUT_ROWS, half), lambda i: (i, 0)),
            pl.BlockSpec((OUT_ROWS, half), lambda i: (i, 0)),
            pl.BlockSpec((OUT_ROWS, ple), lambda i: (i, 0)),
            resident(w_out.shape),
            resident(w_gate.shape),
            resident(w_up.shape),
            resident(final_nw.shape),
        ],
        out_specs=pl.BlockSpec((OUT_ROWS, d_model), lambda i: (i, 0)),
        out_shape=jax.ShapeDtypeStruct((tokens, d_model), F32),
        compiler_params=pltpu.CompilerParams(
            dimension_semantics=("arbitrary",),
            vmem_limit_bytes=VMEM_LIMIT_BYTES),
        name="outproj",
    )(h, ya, yb, p, w_out, w_gate, w_up, final_nw)


def kernel(x, p, norm_w, w_in, dn_conv_w, dn_A_log, dn_dt_bias, dn_norm_w, hg_lb_logits, hg_norm_w,
           w_out, w_ple_up, w_ple_gate, final_norm_w):
    batch, seq, d_model = x.shape
    depth = w_in.shape[0]
    dn_heads = dn_A_log.shape[1]
    dn_width = dn_heads * HEAD_DIM
    hg_width = hg_lb_logits.shape[1]
    hg_heads = hg_width // HEAD_DIM
    tokens = batch * seq
    gate0 = 4 * dn_width
    gate1 = gate0 + 2 * dn_heads
    hg_col0 = gate0 // HEAD_DIM

    sums_np, masks_np = _level_constants()
    sums = jnp.asarray(sums_np, dtype=BF16)
    masks = jnp.asarray(masks_np)

    h = x.reshape(tokens, d_model)
    for l in range(depth):
        w_l = w_in[l]
        w_main = jnp.concatenate([w_l[:, :gate0], w_l[:, gate1:]], axis=1).astype(BF16)
        w_bat = w_l[:, gate0:gate1].T.astype(BF16)
        proj, bat = _inproj(h, norm_w[l][None, :], w_main, w_bat)
        bat3 = bat.reshape(2 * dn_heads, 1, tokens)
        ya = _deltanet(proj, bat3, dn_conv_w[l], dn_A_log[l], dn_dt_bias[l], dn_norm_w[l][None, :],
                       masks, batch, seq, dn_heads)
        yb = _hgrn2(proj, hg_lb_logits, hg_norm_w[l][None, :], sums, masks, l, hg_col0,
                    batch, seq, hg_heads)
        h = _outproj(h, ya, yb, p[l].reshape(tokens, -1), w_out[l].astype(BF16),
                     w_ple_gate[l].astype(BF16), w_ple_up[l].astype(BF16), final_norm_w[None, :],
                     l == depth - 1)
    return h.reshape(batch, seq, d_model)
```

```python
import functools

import jax
import jax.numpy as jnp
import numpy as np
from jax import lax
from jax.experimental import pallas as pl
from jax.experimental.pallas import tpu as pltpu

F32 = jnp.float32
BF16 = jnp.bfloat16

HEAD_DIM = 128
CHUNK = 64
PAIR = 2 * CHUNK
CONV_WIDTH = 4
NORM_EPS = 1e-6
L2_EPS = 1e-6
N_LEVELS = 6
VMEM_LIMIT_BYTES = 56 * 1024 * 1024

IN_ROWS = 1024
IN_COLS = 512
OUT_ROWS = 256
HG_CHUNKS = 8
HG_ROWS = HG_CHUNKS * CHUNK
DN_GROUP = 4
DN_ROWS = 4 * CHUNK


def _bdot(a, b):
    return jnp.dot(a.astype(BF16), b.astype(BF16), preferred_element_type=F32)


def _bdot_nt(a, b):
    return lax.dot_general(a.astype(BF16), b.astype(BF16), (((1,), (1,)), ((), ())),
                           preferred_element_type=F32)


def _bdot_tn(a, b):
    return lax.dot_general(a.astype(BF16), b.astype(BF16), (((0,), (0,)), ((), ())),
                           preferred_element_type=F32)


def _split2(x):
    hi = x.astype(BF16)
    lo = (x - hi.astype(F32)).astype(BF16)
    return hi, lo


def _sigmoid_pair(x):
    e = jnp.exp(-jnp.abs(x))
    r = 1.0 / (1.0 + e)
    er = e * r
    pos = x >= 0
    return jnp.where(pos, r, er), jnp.where(pos, er, r)


def _sigmoid(x):
    return 1.0 / (1.0 + jnp.exp(-x))


def _silu(x):
    return x * _sigmoid(x)


def _softplus(x):
    return jnp.maximum(x, 0.0) + jnp.log1p(jnp.exp(-jnp.abs(x)))


def _gated_head_norm(o, z, w):
    o = o * lax.rsqrt(jnp.mean(o * o, axis=-1, keepdims=True) + NORM_EPS) * w
    return o * _silu(z)


def _inproj_kernel(h_ref, nw_ref, w_ref, wbat_ref, proj_ref, bat_ref, hn_ref):
    @pl.when(pl.program_id(1) == 0)
    def _():
        x = h_ref[...]
        ms = jnp.mean(x * x, axis=-1, keepdims=True)
        hn = (x * lax.rsqrt(ms + NORM_EPS) * nw_ref[...]).astype(BF16)
        hn_ref[...] = hn
        bat_ref[...] = lax.dot_general(wbat_ref[...], hn, (((1,), (1,)), ((), ())),
                                       preferred_element_type=F32)

    proj_ref[...] = jnp.dot(hn_ref[...], w_ref[...], preferred_element_type=F32)


def _inproj(h, nw, w_main, w_bat):
    tokens, d_model = h.shape
    width = w_main.shape[1]
    n_gate = w_bat.shape[0]
    grid = (tokens // IN_ROWS, width // IN_COLS)
    return pl.pallas_call(
        _inproj_kernel,
        grid=grid,
        in_specs=[
            pl.BlockSpec((IN_ROWS, d_model), lambda i, j: (i, 0)),
            pl.BlockSpec((1, d_model), lambda i, j: (0, 0)),
            pl.BlockSpec((d_model, IN_COLS), lambda i, j: (0, j)),
            pl.BlockSpec((n_gate, d_model), lambda i, j: (0, 0)),
        ],
        out_specs=[
            pl.BlockSpec((IN_ROWS, IN_COLS), lambda i, j: (i, j)),
            pl.BlockSpec((n_gate, IN_ROWS), lambda i, j: (0, i)),
        ],
        out_shape=[
            jax.ShapeDtypeStruct((tokens, width), F32),
            jax.ShapeDtypeStruct((n_gate, tokens), F32),
        ],
        scratch_shapes=[pltpu.VMEM((IN_ROWS, d_model), BF16)],
        compiler_params=pltpu.CompilerParams(
            dimension_semantics=("arbitrary", "arbitrary"),
            vmem_limit_bytes=VMEM_LIMIT_BYTES),
        name="inproj",
    )(h, nw, w_main, w_bat)


def _level_constants():
    t = np.arange(CHUNK)
    sums = [(t[None, :] <= t[:, None])]
    masks = [np.eye(CHUNK, dtype=bool)]
    for lvl in range(1, N_LEVELS + 1):
        size, half = 1 << lvl, 1 << (lvl - 1)
        ref = (t // size) * size + half - 1
        sums.append(t[None, :] <= ref[:, None])
        same = (t[:, None] // size) == (t[None, :] // size)
        upper_r = ((t // half) % 2 == 1)[:, None]
        lower_s = ((t // half) % 2 == 0)[None, :]
        masks.append(same & upper_r & lower_s)
    sums.append(np.ones((CHUNK, CHUNK), dtype=bool))
    sums = np.concatenate(sums, axis=0).astype(np.float32)
    sums = np.concatenate([sums, sums], axis=1)
    masks = np.stack(masks, axis=0).astype(np.float32)
    assert masks.sum(axis=0).tolist() == np.tril(np.ones((CHUNK, CHUNK))).tolist()
    return sums, masks


def _pair_constants():
    k = np.arange(2 * PAIR)
    n2 = np.arange(2 * PAIR)
    n1 = np.arange(PAIR)
    k_chunk = (k % PAIR) // CHUNK
    k_pos = k % CHUNK
    col_sel = (k_chunk[:, None] == (n2 // PAIR)[None, :])
    row_cum = (k_chunk[:, None] == (n1 // CHUNK)[None, :]) & (k_pos[:, None] <= (n1 % CHUNK)[None, :])
    return col_sel.astype(np.float32), row_cum.astype(np.float32)


def _conv_silu(x_ref, w_ref, buf_ref, lanes):
    rows = x_ref.shape[0]
    x = x_ref[:, lanes]
    buf_ref[pl.ds(8, rows), lanes] = x
    w = w_ref[:, lanes]
    acc = w[CONV_WIDTH - 1:CONV_WIDTH, :] * x
    for j in range(CONV_WIDTH - 1):
        shift = CONV_WIDTH - 1 - j
        acc = acc + w[j:j + 1, :] * buf_ref[pl.ds(8 - shift, rows), lanes]
    buf_ref[pl.ds(0, 8), lanes] = x[rows - 8:rows, :]
    return _silu(acc)


def _deltanet_kernel(alog_ref, dtb_ref, q_ref, k_ref, v_ref, z_ref, b_ref, a_ref,
                     wq_ref, wk_ref, wv_ref, nw_ref, masks_ref, colsel_ref, rowcum_ref, y_ref,
                     state_ref, qbuf_ref, kbuf_ref, vbuf_ref):
    group = pl.program_id(1)
    width = q_ref.shape[1]

    @pl.when(pl.program_id(2) == 0)
    def _():
        state_ref[...] = jnp.zeros_like(state_ref)
        qbuf_ref[pl.ds(0, 8), :] = jnp.zeros((8, width), F32)
        kbuf_ref[pl.ds(0, 8), :] = jnp.zeros((8, width), F32)
        vbuf_ref[pl.ds(0, 8), :] = jnp.zeros((8, width), F32)

    rows = q_ref.shape[0]
    n_pairs = rows // PAIR
    heads = range(DN_GROUP)

    lane = lax.broadcasted_iota(jnp.int32, (CHUNK, PAIR), 1)
    row = lax.broadcasted_iota(jnp.int32, (CHUNK, PAIR), 0)
    col = lane & (CHUNK - 1)
    left = lane < CHUNK
    causal = col <= row
    strict = col < row
    eye = col == row
    left_b = jnp.where(left, 1.0, 0.0).astype(BF16)
    right_b = jnp.where(left, 0.0, 1.0).astype(BF16)
    level_b = [masks_ref[lvl].astype(BF16) for lvl in range(N_LEVELS + 1)]
    col_sel = colsel_ref[...]
    row_cum = rowcum_ref[...]

    def block_diag(m):
        return jnp.concatenate([m * left_b, m * right_b], axis=0)

    def pair_dot3(x_hi, x_lo, y_hi, y_lo):
        bd_hi = block_diag(y_hi)
        both = jnp.dot(x_hi, jnp.concatenate([bd_hi, block_diag(y_lo)], axis=1),
                       preferred_element_type=F32)
        return (both[:, 0:PAIR] + both[:, PAIR:2 * PAIR]
                + jnp.dot(x_lo, bd_hi, preferred_element_type=F32))

    def prepare(head_ids):
        units = [(j, p) for j in head_ids for p in range(n_pairs)]
        n_units = len(units)
        q_h, k_h, v_h, beta_rows, g_rows = {}, {}, {}, {}, {}
        for j in head_ids:
            sl = slice(j * HEAD_DIM, (j + 1) * HEAD_DIM)
            qj = _conv_silu(q_ref, wq_ref, qbuf_ref, sl)
            kj = _conv_silu(k_ref, wk_ref, kbuf_ref, sl)
            v_h[j] = _conv_silu(v_ref, wv_ref, vbuf_ref, sl)
            q_h[j] = qj * lax.rsqrt(jnp.sum(qj * qj, axis=-1, keepdims=True) + L2_EPS) * (HEAD_DIM ** -0.5)
            k_h[j] = kj * lax.rsqrt(jnp.sum(kj * kj, axis=-1, keepdims=True) + L2_EPS)
            head = group * DN_GROUP + j
            beta_rows[j] = _sigmoid(b_ref[j:j + 1, :])
            rate = jnp.exp(jnp.full((1, rows), alog_ref[head], F32))
            g_rows[j] = -rate * _softplus(a_ref[j:j + 1, :] + dtb_ref[head])

        cols, g_rowcums = [], []
        for j, p in units:
            ps = slice(p * PAIR, (p + 1) * PAIR)
            g_b = jnp.broadcast_to(g_rows[j][:, ps], (CHUNK, PAIR))
            beta_b = jnp.broadcast_to(beta_rows[j][:, ps], (CHUNK, PAIR))
            lhs = jnp.concatenate([jnp.where(causal, g_b, 0.0), jnp.where(eye, beta_b, 0.0), g_b], axis=0)
            hi, lo = _split2(lhs)
            hilo = jnp.concatenate([hi, lo], axis=1)
            cols.append(jnp.dot(hilo, col_sel, preferred_element_type=F32))
            g_rowcums.append(jnp.dot(hilo[2 * CHUNK:3 * CHUNK], row_cum,
                                     preferred_element_type=F32))

        def stacked(u, block):
            c = cols[u][block * CHUNK:(block + 1) * CHUNK]
            return jnp.concatenate([c[:, 0:HEAD_DIM], c[:, HEAD_DIM:2 * HEAD_DIM]], axis=0)

        g_cols = [stacked(u, 0) for u in range(n_units)]
        beta_cols = [stacked(u, 1) for u in range(n_units)]
        g_lasts = [stacked(u, 2) for u in range(n_units)]
        exp_gs = [jnp.exp(g) for g in g_cols]
        q_u = [q_h[j][p * PAIR:(p + 1) * PAIR] for j, p in units]
        k_u = [k_h[j][p * PAIR:(p + 1) * PAIR] for j, p in units]
        v_u = [v_h[j][p * PAIR:(p + 1) * PAIR] for j, p in units]
        k_betas = [k_u[u] * beta_cols[u] for u in range(n_units)]

        prods = [_bdot_nt(jnp.concatenate([k_betas[u], q_u[u]], axis=0), k_u[u]) for u in range(n_units)]

        a_his, a_los, qks, invs = [], [], [], []
        for u in range(n_units):
            g_pair = jnp.where(left, g_cols[u][0:CHUNK], g_cols[u][CHUNK:PAIR])
            diff = g_pair - g_rowcums[u]
            decay = jnp.where(causal, jnp.exp(jnp.where(causal, diff, 0.0)), 0.0)
            pr = prods[u]
            kk = jnp.where(left, pr[0:CHUNK], pr[CHUNK:PAIR])
            qk = jnp.where(left, pr[PAIR:PAIR + CHUNK], pr[PAIR + CHUNK:2 * PAIR])
            a_mat = jnp.where(strict, kk * decay, 0.0)
            a_hi, a_lo = _split2(a_mat)
            a_his.append(a_hi)
            a_los.append(a_lo)
            qks.append(qk * decay)
            invs.append(jnp.where(eye, 1.0, 0.0) - masks_ref[1] * a_mat)

        for lvl in range(2, N_LEVELS + 1):
            t_split = [_split2(t) for t in invs]
            xs = [pair_dot3(a_his[u] * level_b[lvl], a_los[u] * level_b[lvl], *t_split[u])
                  for u in range(n_units)]
            invs = [invs[u] - pair_dot3(*t_split[u], *_split2(xs[u])) for u in range(n_units)]

        sols, qk_sols, kt_sols = [], [], []
        for u in range(n_units):
            rhs = jnp.concatenate([v_u[u] * beta_cols[u], k_betas[u] * exp_gs[u]], axis=1)
            sols.append(jnp.dot(block_diag(invs[u].astype(BF16)), rhs.astype(BF16),
                                preferred_element_type=F32))
        for u in range(n_units):
            qk_sols.append(jnp.dot(block_diag(qks[u].astype(BF16)), sols[u].astype(BF16),
                                   preferred_element_type=F32))
        for u in range(n_units):
            k_tail = k_u[u] * jnp.exp(g_lasts[u] - g_cols[u])
            kt_sols.append([_bdot_tn(k_tail[c * CHUNK:(c + 1) * CHUNK], sols[u][c * CHUNK:(c + 1) * CHUNK])
                            for c in range(2)])

        return {unit: dict(q_decay=q_u[u] * exp_gs[u], qk_sol=qk_sols[u], kt_sol=kt_sols[u],
                           g_last=g_lasts[u])
                for u, unit in enumerate(units)}

    prep = prepare(list(heads))

    states = [state_ref[j] for j in heads]
    outs = [[] for _ in heads]
    for p in range(n_pairs):
        for c in range(2):
            cs = slice(c * CHUNK, (c + 1) * CHUNK)
            for j in heads:
                d = prep[(j, p)]
                kt_sol = d["kt_sol"][c]
                lhs = jnp.concatenate(
                    [-kt_sol[:, HEAD_DIM:2 * HEAD_DIM],
                     d["q_decay"][cs] - d["qk_sol"][cs, HEAD_DIM:2 * HEAD_DIM]], axis=0)
                both = _bdot(lhs, states[j])
                tail = jnp.exp(d["g_last"][c * CHUNK:c * CHUNK + 1, :])
                states[j] = tail * states[j] + both[0:HEAD_DIM] + kt_sol[:, 0:HEAD_DIM]
                outs[j].append(both[HEAD_DIM:HEAD_DIM + CHUNK] + d["qk_sol"][cs, 0:HEAD_DIM])

    for j in heads:
        state_ref[j] = states[j]
        sl = slice(j * HEAD_DIM, (j + 1) * HEAD_DIM)
        o = jnp.concatenate(outs[j], axis=0)
        y_ref[:, sl] = _gated_head_norm(o, z_ref[:, sl], nw_ref[...]).astype(y_ref.dtype)


def _deltanet(proj, bat4, conv_w, a_log, dt_bias, norm_w, masks_pair, col_sel, row_cum, batch, seq, n_heads):
    tokens = proj.shape[0]
    steps = seq // DN_ROWS
    groups = n_heads // DN_GROUP
    width = DN_GROUP * HEAD_DIM
    row_map = lambda col0: (lambda b, g, t: (b * steps + t, col0 + g))
    gate_map = lambda kind: (lambda b, g, t: (kind, g, 0, b * steps + t))
    conv_map = lambda col0: (lambda b, g, t: (0, col0 + g))
    const2 = lambda b, g, t: (0, 0)
    smem = pl.BlockSpec(memory_space=pltpu.SMEM)
    return pl.pallas_call(
        _deltanet_kernel,
        grid=(batch, groups, steps),
        in_specs=[
            smem, smem,
            pl.BlockSpec((DN_ROWS, width), row_map(0)),
            pl.BlockSpec((DN_ROWS, width), row_map(groups)),
            pl.BlockSpec((DN_ROWS, width), row_map(2 * groups)),
            pl.BlockSpec((DN_ROWS, width), row_map(3 * groups)),
            pl.BlockSpec((None, None, DN_GROUP, DN_ROWS), gate_map(0)),
            pl.BlockSpec((None, None, DN_GROUP, DN_ROWS), gate_map(1)),
            pl.BlockSpec((CONV_WIDTH, width), conv_map(0)),
            pl.BlockSpec((CONV_WIDTH, width), conv_map(groups)),
            pl.BlockSpec((CONV_WIDTH, width), conv_map(2 * groups)),
            pl.BlockSpec((1, HEAD_DIM), const2),
            pl.BlockSpec(masks_pair.shape, lambda b, g, t: (0, 0, 0)),
            pl.BlockSpec(col_sel.shape, const2),
            pl.BlockSpec(row_cum.shape, const2),
        ],
        out_specs=pl.BlockSpec((DN_ROWS, width), lambda b, g, t: (b * steps + t, g)),
        out_shape=jax.ShapeDtypeStruct((tokens, n_heads * HEAD_DIM), BF16),
        scratch_shapes=[
            pltpu.VMEM((DN_GROUP, HEAD_DIM, HEAD_DIM), F32),
            pltpu.VMEM((DN_ROWS + 8, width), F32),
            pltpu.VMEM((DN_ROWS + 8, width), F32),
            pltpu.VMEM((DN_ROWS + 8, width), F32),
        ],
        compiler_params=pltpu.CompilerParams(
            dimension_semantics=("arbitrary", "arbitrary", "arbitrary"),
            vmem_limit_bytes=VMEM_LIMIT_BYTES),
        name="deltanet",
    )(a_log, dt_bias, proj, proj, proj, proj, bat4, bat4, conv_w, conv_w, conv_w, norm_w,
      masks_pair, col_sel, row_cum)


def _hgrn2_kernel(layer, q_ref, f_ref, i_ref, z_ref, lbl_ref, nw_ref, sums_ref, masks_ref, y_ref,
                  state_ref):
    @pl.when(pl.program_id(2) == 0)
    def _():
        state_ref[...] = jnp.zeros_like(state_ref)

    logits = lbl_ref[...]
    ex = jnp.exp(logits - jnp.max(logits, axis=0, keepdims=True))
    probs = ex / jnp.sum(ex, axis=0, keepdims=True)
    lb = jnp.sum(probs[0:layer + 1], axis=0, keepdims=True) - probs[0:1]

    sig, nsig = _sigmoid_pair(f_ref[...])
    log_f = jnp.log(lb + (1.0 - lb) * sig)
    k = (1.0 - lb) * nsig
    q = _silu(q_ref[...])
    v = i_ref[...]

    sums2 = sums_ref[...]

    rows = q.shape[0]
    chunks = range(rows // CHUNK)
    sls = [slice(c * CHUNK, (c + 1) * CHUNK) for c in chunks]

    g_alls = []
    for c in chunks:
        hi, lo = _split2(log_f[sls[c]])
        g_alls.append(jnp.dot(sums2, jnp.concatenate([hi, lo], axis=0), preferred_element_type=F32))
    gs = [g_alls[c][0:CHUNK] for c in chunks]
    g_lasts = [g_alls[c][(N_LEVELS + 1) * CHUNK:(N_LEVELS + 2) * CHUNK] for c in chunks]

    a_mats = [masks_ref[0] * _bdot_nt(q[sls[c]], k[sls[c]]) for c in chunks]
    for lvl in range(1, N_LEVELS + 1):
        for c in chunks:
            g_ref = g_alls[c][lvl * CHUNK:(lvl + 1) * CHUNK]
            wgt = jnp.exp(-jnp.abs(gs[c] - g_ref))
            a_mats[c] = a_mats[c] + masks_ref[lvl] * _bdot_nt(q[sls[c]] * wgt, k[sls[c]] * wgt)

    incs = [_bdot_tn(v[sls[c]], k[sls[c]] * jnp.exp(g_lasts[c] - gs[c])) for c in chunks]
    intras = [_bdot(a_mats[c], v[sls[c]]) for c in chunks]

    state_t = state_ref[...]
    outs = []
    for c in chunks:
        outs.append(_bdot_nt(q[sls[c]] * jnp.exp(gs[c]), state_t) + intras[c])
        state_t = jnp.exp(g_lasts[c][0:1, :]) * state_t + incs[c]

    state_ref[...] = state_t
    o = jnp.concatenate(outs, axis=0)
    y_ref[...] = _gated_head_norm(o, z_ref[...], nw_ref[...]).astype(y_ref.dtype)


def _hgrn2(proj, lb_logits, norm_w, sums, masks, layer, col0, batch, seq, n_heads):
    tokens = proj.shape[0]
    depth = lb_logits.shape[0]
    steps = seq // HG_ROWS
    row_map = lambda c0: (lambda b, h, t: (b * steps + t, c0 + h))
    return pl.pallas_call(
        functools.partial(_hgrn2_kernel, layer),
        grid=(batch, n_heads, steps),
        in_specs=[
            pl.BlockSpec((HG_ROWS, HEAD_DIM), row_map(col0)),
            pl.BlockSpec((HG_ROWS, HEAD_DIM), row_map(col0 + n_heads)),
            pl.BlockSpec((HG_ROWS, HEAD_DIM), row_map(col0 + 2 * n_heads)),
            pl.BlockSpec((HG_ROWS, HEAD_DIM), row_map(col0 + 3 * n_heads)),
            pl.BlockSpec((depth, HEAD_DIM), lambda b, h, t: (0, h)),
            pl.BlockSpec((1, HEAD_DIM), lambda b, h, t: (0, 0)),
            pl.BlockSpec(sums.shape, lambda b, h, t: (0, 0)),
            pl.BlockSpec(masks.shape, lambda b, h, t: (0, 0, 0)),
        ],
        out_specs=pl.BlockSpec((HG_ROWS, HEAD_DIM), lambda b, h, t: (b * steps + t, h)),
        out_shape=jax.ShapeDtypeStruct((tokens, n_heads * HEAD_DIM), BF16),
        scratch_shapes=[pltpu.VMEM((HEAD_DIM, HEAD_DIM), F32)],
        compiler_params=pltpu.CompilerParams(
            dimension_semantics=("arbitrary", "arbitrary", "arbitrary"),
            vmem_limit_bytes=VMEM_LIMIT_BYTES),
        name="hgrn2",
    )(proj, proj, proj, proj, lb_logits, norm_w, sums, masks)


def _outproj_kernel(final, h_ref, ya_ref, yb_ref, p_ref, wo_ref, wg_ref, wu_ref, nw_ref, o_ref):
    y = jnp.concatenate([ya_ref[...], yb_ref[...]], axis=1)
    h1 = h_ref[...] + jnp.dot(y, wo_ref[...], preferred_element_type=F32)
    gate = _sigmoid(jnp.dot(h1.astype(BF16), wg_ref[...], preferred_element_type=F32))
    up = jnp.dot(p_ref[...].astype(BF16), wu_ref[...], preferred_element_type=F32)
    h2 = h1 + up * gate
    if final:
        ms = jnp.mean(h2 * h2, axis=-1, keepdims=True)
        h2 = h2 * lax.rsqrt(ms + NORM_EPS) * nw_ref[...]
    o_ref[...] = h2


def _outproj(h, ya, yb, p, w_out, w_gate, w_up, final_nw, final):
    tokens, d_model = h.shape
    half = ya.shape[1]
    ple = p.shape[1]
    resident = lambda shape: pl.BlockSpec(shape, lambda i: (0, 0), pipeline_mode=pl.Buffered(1))
    return pl.pallas_call(
        functools.partial(_outproj_kernel, final),
        grid=(tokens // OUT_ROWS,),
        in_specs=[
            pl.BlockSpec((OUT_ROWS, d_model), lambda i: (i, 0)),
            pl.BlockSpec((OUT_ROWS, half), lambda i: (i, 0)),
            pl.BlockSpec((OUT_ROWS, half), lambda i: (i, 0)),
            pl.BlockSpec((OUT_ROWS, ple), lambda i: (i, 0)),
            resident(w_out.shape),
            resident(w_gate.shape),
            resident(w_up.shape),
            resident(final_nw.shape),
        ],
        out_specs=pl.BlockSpec((OUT_ROWS, d_model), lambda i: (i, 0)),
        out_shape=jax.ShapeDtypeStruct((tokens, d_model), F32),
        compiler_params=pltpu.CompilerParams(
            dimension_semantics=("arbitrary",),
            vmem_limit_bytes=VMEM_LIMIT_BYTES),
        name="outproj",
    )(h, ya, yb, p, w_out, w_gate, w_up, final_nw)


def kernel(x, p, norm_w, w_in, dn_conv_w, dn_A_log, dn_dt_bias, dn_norm_w, hg_lb_logits, hg_norm_w,
           w_out, w_ple_up, w_ple_gate, final_norm_w):
    batch, seq, d_model = x.shape
    depth = w_in.shape[0]
    dn_heads = dn_A_log.shape[1]
    dn_width = dn_heads * HEAD_DIM
    hg_width = hg_lb_logits.shape[1]
    hg_heads = hg_width // HEAD_DIM
    tokens = batch * seq
    gate0 = 4 * dn_width
    gate1 = gate0 + 2 * dn_heads
    hg_col0 = gate0 // HEAD_DIM
    assert dn_heads % DN_GROUP == 0 and seq % DN_ROWS == 0 and seq % HG_ROWS == 0

    sums_np, masks_np = _level_constants()
    col_sel_np, row_cum_np = _pair_constants()
    sums = jnp.asarray(sums_np, dtype=BF16)
    masks = jnp.asarray(masks_np)
    masks_pair = jnp.asarray(np.concatenate([masks_np, masks_np], axis=2))
    col_sel = jnp.asarray(col_sel_np, dtype=BF16)
    row_cum = jnp.asarray(row_cum_np, dtype=BF16)

    h = x.reshape(tokens, d_model)
    for l in range(depth):
        w_l = w_in[l]
        w_main = jnp.concatenate([w_l[:, :gate0], w_l[:, gate1:]], axis=1).astype(BF16)
        w_bat = w_l[:, gate0:gate1].T.astype(BF16)
        proj, bat = _inproj(h, norm_w[l][None, :], w_main, w_bat)
        bat4 = bat.reshape(2, dn_heads // DN_GROUP, DN_GROUP, tokens)
        ya = _deltanet(proj, bat4, dn_conv_w[l], dn_A_log[l], dn_dt_bias[l], dn_norm_w[l][None, :],
                       masks_pair, col_sel, row_cum, batch, seq, dn_heads)
        yb = _hgrn2(proj, hg_lb_logits, hg_norm_w[l][None, :], sums, masks, l, hg_col0,
                    batch, seq, hg_heads)
        h = _outproj(h, ya, yb, p[l].reshape(tokens, -1), w_out[l].astype(BF16),
                     w_ple_gate[l].astype(BF16), w_ple_up[l].astype(BF16), final_norm_w[None, :],
                     l == depth - 1)
    return h.reshape(batch, seq, d_model)
```

```python
import functools

import jax
import jax.numpy as jnp
import numpy as np
from jax import lax
from jax.experimental import pallas as pl
from jax.experimental.pallas import tpu as pltpu

F32 = jnp.float32
BF16 = jnp.bfloat16

HEAD_DIM = 128
CHUNK = 64
PAIR = 2 * CHUNK
CONV_WIDTH = 4
NORM_EPS = 1e-6
L2_EPS = 1e-6
N_LEVELS = 6
VMEM_LIMIT_BYTES = 56 * 1024 * 1024

IN_ROWS = 1024
IN_COLS = 1024
OUT_ROWS = 256
HG_CHUNKS = 8
HG_ROWS = HG_CHUNKS * CHUNK
DN_GROUP = 4
DN_ROWS = 4 * CHUNK


def _bdot(a, b):
    return jnp.dot(a.astype(BF16), b.astype(BF16), preferred_element_type=F32)


def _bdot_nt(a, b):
    return lax.dot_general(a.astype(BF16), b.astype(BF16), (((1,), (1,)), ((), ())),
                           preferred_element_type=F32)


def _bdot_tn(a, b):
    return lax.dot_general(a.astype(BF16), b.astype(BF16), (((0,), (0,)), ((), ())),
                           preferred_element_type=F32)


def _split2(x):
    hi = x.astype(BF16)
    lo = (x - hi.astype(F32)).astype(BF16)
    return hi, lo


def _sigmoid_pair(x):
    e = jnp.exp(-jnp.abs(x))
    r = 1.0 / (1.0 + e)
    er = e * r
    pos = x >= 0
    return jnp.where(pos, r, er), jnp.where(pos, er, r)


def _sigmoid(x):
    return 1.0 / (1.0 + jnp.exp(-x))


def _silu(x):
    return x * _sigmoid(x)


def _softplus(x):
    return jnp.maximum(x, 0.0) + jnp.log1p(jnp.exp(-jnp.abs(x)))


def _gated_head_norm(o, z, w):
    o = o * lax.rsqrt(jnp.mean(o * o, axis=-1, keepdims=True) + NORM_EPS) * w
    return o * _silu(z)


def _inproj_kernel(n_first, h_ref, nw_ref, wa_ref, wb_ref, wbat_ref, proj_ref, bat_ref, hn_ref):
    j = pl.program_id(1)
    contract_last = (((1,), (1,)), ((), ()))

    @pl.when(j == 0)
    def _():
        x = h_ref[...]
        ms = jnp.mean(x * x, axis=-1, keepdims=True)
        hn = (x * lax.rsqrt(ms + NORM_EPS) * nw_ref[...]).astype(BF16)
        hn_ref[...] = hn
        bat_ref[...] = lax.dot_general(wbat_ref[...], hn, contract_last, preferred_element_type=F32)

    @pl.when(j < n_first)
    def _():
        proj_ref[...] = lax.dot_general(hn_ref[...], wa_ref[...], contract_last,
                                        preferred_element_type=F32)

    @pl.when(j >= n_first)
    def _():
        proj_ref[...] = lax.dot_general(hn_ref[...], wb_ref[...], contract_last,
                                        preferred_element_type=F32)


def _inproj(h, nw, wt_a, wt_b, w_bat):
    tokens, d_model = h.shape
    n_first = wt_a.shape[0] // IN_COLS
    n_second = wt_b.shape[0] // IN_COLS
    width = wt_a.shape[0] + wt_b.shape[0]
    n_gate = w_bat.shape[0]
    grid = (tokens // IN_ROWS, n_first + n_second)
    return pl.pallas_call(
        functools.partial(_inproj_kernel, n_first),
        grid=grid,
        in_specs=[
            pl.BlockSpec((IN_ROWS, d_model), lambda i, j: (i, 0)),
            pl.BlockSpec((1, d_model), lambda i, j: (0, 0)),
            pl.BlockSpec((IN_COLS, d_model), lambda i, j: (jnp.minimum(j, n_first - 1), 0)),
            pl.BlockSpec((IN_COLS, d_model), lambda i, j: (jnp.maximum(j - n_first, 0), 0)),
            pl.BlockSpec((n_gate, d_model), lambda i, j: (0, 0)),
        ],
        out_specs=[
            pl.BlockSpec((IN_ROWS, IN_COLS), lambda i, j: (i, j)),
            pl.BlockSpec((n_gate, IN_ROWS), lambda i, j: (0, i)),
        ],
        out_shape=[
            jax.ShapeDtypeStruct((tokens, width), F32),
            jax.ShapeDtypeStruct((n_gate, tokens), F32),
        ],
        scratch_shapes=[pltpu.VMEM((IN_ROWS, d_model), BF16)],
        compiler_params=pltpu.CompilerParams(
            dimension_semantics=("arbitrary", "arbitrary"),
            vmem_limit_bytes=VMEM_LIMIT_BYTES),
        name="inproj",
    )(h, nw, wt_a, wt_b, w_bat)


def _level_constants():
    t = np.arange(CHUNK)
    sums = [(t[None, :] <= t[:, None])]
    masks = [np.eye(CHUNK, dtype=bool)]
    for lvl in range(1, N_LEVELS + 1):
        size, half = 1 << lvl, 1 << (lvl - 1)
        ref = (t // size) * size + half - 1
        sums.append(t[None, :] <= ref[:, None])
        same = (t[:, None] // size) == (t[None, :] // size)
        upper_r = ((t // half) % 2 == 1)[:, None]
        lower_s = ((t // half) % 2 == 0)[None, :]
        masks.append(same & upper_r & lower_s)
    sums.append(np.ones((CHUNK, CHUNK), dtype=bool))
    sums = np.concatenate(sums, axis=0).astype(np.float32)
    sums = np.concatenate([sums, sums], axis=1)
    masks = np.stack(masks, axis=0).astype(np.float32)
    assert masks.sum(axis=0).tolist() == np.tril(np.ones((CHUNK, CHUNK))).tolist()
    return sums, masks


def _pair_constants():
    k = np.arange(2 * PAIR)
    n2 = np.arange(2 * PAIR)
    n1 = np.arange(PAIR)
    k_chunk = (k % PAIR) // CHUNK
    k_pos = k % CHUNK
    col_sel = (k_chunk[:, None] == (n2 // PAIR)[None, :])
    row_cum = (k_chunk[:, None] == (n1 // CHUNK)[None, :]) & (k_pos[:, None] <= (n1 % CHUNK)[None, :])
    return col_sel.astype(np.float32), row_cum.astype(np.float32)


def _conv_silu(x_ref, w_ref, buf_ref, lanes):
    rows = x_ref.shape[0]
    x = x_ref[:, lanes]
    buf_ref[pl.ds(8, rows), lanes] = x
    w = w_ref[:, lanes]
    acc = w[CONV_WIDTH - 1:CONV_WIDTH, :] * x
    for j in range(CONV_WIDTH - 1):
        shift = CONV_WIDTH - 1 - j
        acc = acc + w[j:j + 1, :] * buf_ref[pl.ds(8 - shift, rows), lanes]
    buf_ref[pl.ds(0, 8), lanes] = x[rows - 8:rows, :]
    return _silu(acc)


def _deltanet_kernel(alog_ref, dtb_ref, q_ref, k_ref, v_ref, z_ref, b_ref, a_ref,
                     wq_ref, wk_ref, wv_ref, nw_ref, masks_ref, colsel_ref, rowcum_ref, y_ref,
                     state_ref, qbuf_ref, kbuf_ref, vbuf_ref):
    group = pl.program_id(1)
    width = q_ref.shape[1]

    @pl.when(pl.program_id(2) == 0)
    def _():
        state_ref[...] = jnp.zeros_like(state_ref)
        qbuf_ref[pl.ds(0, 8), :] = jnp.zeros((8, width), F32)
        kbuf_ref[pl.ds(0, 8), :] = jnp.zeros((8, width), F32)
        vbuf_ref[pl.ds(0, 8), :] = jnp.zeros((8, width), F32)

    rows = q_ref.shape[0]
    n_pairs = rows // PAIR
    heads = range(DN_GROUP)
    units = [(j, p) for j in heads for p in range(n_pairs)]
    n_units = len(units)

    lane = lax.broadcasted_iota(jnp.int32, (CHUNK, PAIR), 1)
    row = lax.broadcasted_iota(jnp.int32, (CHUNK, PAIR), 0)
    col = lane & (CHUNK - 1)
    left = lane < CHUNK
    causal = col <= row
    strict = col < row
    eye = col == row
    left_b = jnp.where(left, 1.0, 0.0).astype(BF16)
    right_b = jnp.where(left, 0.0, 1.0).astype(BF16)
    level_b = [masks_ref[lvl].astype(BF16) for lvl in range(N_LEVELS + 1)]
    col_sel = colsel_ref[...]
    row_cum = rowcum_ref[...]

    def block_diag(m):
        return jnp.concatenate([m * left_b, m * right_b], axis=0)

    def pair_dot3(x_hi, x_lo, y_hi, y_lo):
        bd_hi = block_diag(y_hi)
        both = jnp.dot(x_hi, jnp.concatenate([bd_hi, block_diag(y_lo)], axis=1),
                       preferred_element_type=F32)
        return (both[:, 0:PAIR] + both[:, PAIR:2 * PAIR]
                + jnp.dot(x_lo, bd_hi, preferred_element_type=F32))

    q_h, k_h, v_h, beta_rows, g_rows = [], [], [], [], []
    for j in heads:
        sl = slice(j * HEAD_DIM, (j + 1) * HEAD_DIM)
        qj = _conv_silu(q_ref, wq_ref, qbuf_ref, sl)
        kj = _conv_silu(k_ref, wk_ref, kbuf_ref, sl)
        v_h.append(_conv_silu(v_ref, wv_ref, vbuf_ref, sl))
        q_h.append(qj * lax.rsqrt(jnp.sum(qj * qj, axis=-1, keepdims=True) + L2_EPS) * (HEAD_DIM ** -0.5))
        k_h.append(kj * lax.rsqrt(jnp.sum(kj * kj, axis=-1, keepdims=True) + L2_EPS))
        head = group * DN_GROUP + j
        beta_rows.append(_sigmoid(b_ref[j:j + 1, :]))
        rate = jnp.exp(jnp.full((1, rows), alog_ref[head], F32))
        g_rows.append(-rate * _softplus(a_ref[j:j + 1, :] + dtb_ref[head]))

    cols, g_rowcums = [], []
    for j, p in units:
        ps = slice(p * PAIR, (p + 1) * PAIR)
        g_b = jnp.broadcast_to(g_rows[j][:, ps], (CHUNK, PAIR))
        beta_b = jnp.broadcast_to(beta_rows[j][:, ps], (CHUNK, PAIR))
        lhs = jnp.concatenate([jnp.where(causal, g_b, 0.0), jnp.where(eye, beta_b, 0.0), g_b], axis=0)
        hi, lo = _split2(lhs)
        hilo = jnp.concatenate([hi, lo], axis=1)
        cols.append(jnp.dot(hilo, col_sel, preferred_element_type=F32))
        g_rowcums.append(jnp.dot(hilo[2 * CHUNK:3 * CHUNK], row_cum,
                                 preferred_element_type=F32))

    def stacked(u, block):
        c = cols[u][block * CHUNK:(block + 1) * CHUNK]
        return jnp.concatenate([c[:, 0:HEAD_DIM], c[:, HEAD_DIM:2 * HEAD_DIM]], axis=0)

    g_cols = [stacked(u, 0) for u in range(n_units)]
    beta_cols = [stacked(u, 1) for u in range(n_units)]
    g_lasts = [stacked(u, 2) for u in range(n_units)]
    exp_gs = [jnp.exp(g) for g in g_cols]
    q_u = [q_h[j][p * PAIR:(p + 1) * PAIR] for j, p in units]
    k_u = [k_h[j][p * PAIR:(p + 1) * PAIR] for j, p in units]
    v_u = [v_h[j][p * PAIR:(p + 1) * PAIR] for j, p in units]
    k_betas = [k_u[u] * beta_cols[u] for u in range(n_units)]

    prods = [_bdot_nt(jnp.concatenate([k_betas[u], q_u[u]], axis=0), k_u[u]) for u in range(n_units)]

    a_his, a_los, qks, invs = [], [], [], []
    for u in range(n_units):
        g_pair = jnp.where(left, g_cols[u][0:CHUNK], g_cols[u][CHUNK:PAIR])
        diff = g_pair - g_rowcums[u]
        decay = jnp.where(causal, jnp.exp(jnp.where(causal, diff, 0.0)), 0.0)
        pr = prods[u]
        kk = jnp.where(left, pr[0:CHUNK], pr[CHUNK:PAIR])
        qk = jnp.where(left, pr[PAIR:PAIR + CHUNK], pr[PAIR + CHUNK:2 * PAIR])
        a_mat = jnp.where(strict, kk * decay, 0.0)
        a_hi, a_lo = _split2(a_mat)
        a_his.append(a_hi)
        a_los.append(a_lo)
        qks.append(qk * decay)
        invs.append(jnp.where(eye, 1.0, 0.0) - masks_ref[1] * a_mat)

    for lvl in range(2, N_LEVELS + 1):
        t_split = [_split2(t) for t in invs]
        xs = [pair_dot3(a_his[u] * level_b[lvl], a_los[u] * level_b[lvl], *t_split[u])
              for u in range(n_units)]
        invs = [invs[u] - pair_dot3(*t_split[u], *_split2(xs[u])) for u in range(n_units)]

    sols, qk_sols, kt_sols = [], [], []
    for u in range(n_units):
        rhs = jnp.concatenate([v_u[u] * beta_cols[u], k_betas[u] * exp_gs[u]], axis=1)
        sols.append(jnp.dot(block_diag(invs[u].astype(BF16)), rhs.astype(BF16),
                            preferred_element_type=F32))
    for u in range(n_units):
        qk_sols.append(jnp.dot(block_diag(qks[u].astype(BF16)), sols[u].astype(BF16),
                               preferred_element_type=F32))
    for u in range(n_units):
        k_tail = k_u[u] * jnp.exp(g_lasts[u] - g_cols[u])
        kt_sols.append([_bdot_tn(k_tail[c * CHUNK:(c + 1) * CHUNK], sols[u][c * CHUNK:(c + 1) * CHUNK])
                        for c in range(2)])

    states = [state_ref[j] for j in heads]
    outs = [[] for _ in heads]
    for p in range(n_pairs):
        for c in range(2):
            cs = slice(c * CHUNK, (c + 1) * CHUNK)
            for j in heads:
                u = j * n_pairs + p
                kt_sol = kt_sols[u][c]
                lhs = jnp.concatenate(
                    [-kt_sol[:, HEAD_DIM:2 * HEAD_DIM],
                     q_u[u][cs] * exp_gs[u][cs] - qk_sols[u][cs, HEAD_DIM:2 * HEAD_DIM]], axis=0)
                both = _bdot(lhs, states[j])
                tail = jnp.exp(g_lasts[u][c * CHUNK:c * CHUNK + 1, :])
                states[j] = tail * states[j] + both[0:HEAD_DIM] + kt_sol[:, 0:HEAD_DIM]
                outs[j].append(both[HEAD_DIM:HEAD_DIM + CHUNK] + qk_sols[u][cs, 0:HEAD_DIM])

    for j in heads:
        state_ref[j] = states[j]
        sl = slice(j * HEAD_DIM, (j + 1) * HEAD_DIM)
        o = jnp.concatenate(outs[j], axis=0)
        y_ref[:, sl] = _gated_head_norm(o, z_ref[:, sl], nw_ref[...]).astype(y_ref.dtype)


def _deltanet(proj, bat4, conv_w, a_log, dt_bias, norm_w, masks_pair, col_sel, row_cum, batch, seq, n_heads):
    tokens = proj.shape[0]
    steps = seq // DN_ROWS
    groups = n_heads // DN_GROUP
    width = DN_GROUP * HEAD_DIM
    row_map = lambda col0: (lambda b, g, t: (b * steps + t, col0 + g))
    gate_map = lambda kind: (lambda b, g, t: (kind, g, 0, b * steps + t))
    conv_map = lambda col0: (lambda b, g, t: (0, col0 + g))
    const2 = lambda b, g, t: (0, 0)
    smem = pl.BlockSpec(memory_space=pltpu.SMEM)
    return pl.pallas_call(
        _deltanet_kernel,
        grid=(batch, groups, steps),
        in_specs=[
            smem, smem,
            pl.BlockSpec((DN_ROWS, width), row_map(0)),
            pl.BlockSpec((DN_ROWS, width), row_map(groups)),
            pl.BlockSpec((DN_ROWS, width), row_map(2 * groups)),
            pl.BlockSpec((DN_ROWS, width), row_map(3 * groups)),
            pl.BlockSpec((None, None, DN_GROUP, DN_ROWS), gate_map(0)),
            pl.BlockSpec((None, None, DN_GROUP, DN_ROWS), gate_map(1)),
            pl.BlockSpec((CONV_WIDTH, width), conv_map(0)),
            pl.BlockSpec((CONV_WIDTH, width), conv_map(groups)),
            pl.BlockSpec((CONV_WIDTH, width), conv_map(2 * groups)),
            pl.BlockSpec((1, HEAD_DIM), const2),
            pl.BlockSpec(masks_pair.shape, lambda b, g, t: (0, 0, 0)),
            pl.BlockSpec(col_sel.shape, const2),
            pl.BlockSpec(row_cum.shape, const2),
        ],
        out_specs=pl.BlockSpec((DN_ROWS, width), lambda b, g, t: (b * steps + t, g)),
        out_shape=jax.ShapeDtypeStruct((tokens, n_heads * HEAD_DIM), BF16),
        scratch_shapes=[
            pltpu.VMEM((DN_GROUP, HEAD_DIM, HEAD_DIM), F32),
            pltpu.VMEM((DN_ROWS + 8, width), F32),
            pltpu.VMEM((DN_ROWS + 8, width), F32),
            pltpu.VMEM((DN_ROWS + 8, width), F32),
        ],
        compiler_params=pltpu.CompilerParams(
            dimension_semantics=("arbitrary", "arbitrary", "arbitrary"),
            vmem_limit_bytes=VMEM_LIMIT_BYTES),
        name="deltanet",
    )(a_log, dt_bias, proj, proj, proj, proj, bat4, bat4, conv_w, conv_w, conv_w, norm_w,
      masks_pair, col_sel, row_cum)


def _hgrn2_kernel(layer, q_ref, f_ref, i_ref, z_ref, lbl_ref, nw_ref, sums_ref, masks_ref, y_ref,
                  state_ref):
    @pl.when(pl.program_id(2) == 0)
    def _():
        state_ref[...] = jnp.zeros_like(state_ref)

    logits = lbl_ref[...]
    ex = jnp.exp(logits - jnp.max(logits, axis=0, keepdims=True))
    probs = ex / jnp.sum(ex, axis=0, keepdims=True)
    lb = jnp.sum(probs[0:layer + 1], axis=0, keepdims=True) - probs[0:1]

    sig, nsig = _sigmoid_pair(f_ref[...])
    log_f = jnp.log(lb + (1.0 - lb) * sig)
    k = (1.0 - lb) * nsig
    q = _silu(q_ref[...])
    v = i_ref[...]

    sums2 = sums_ref[...]

    rows = q.shape[0]
    chunks = range(rows // CHUNK)
    sls = [slice(c * CHUNK, (c + 1) * CHUNK) for c in chunks]

    g_alls = []
    for c in chunks:
        hi, lo = _split2(log_f[sls[c]])
        g_alls.append(jnp.dot(sums2, jnp.concatenate([hi, lo], axis=0), preferred_element_type=F32))
    gs = [g_alls[c][0:CHUNK] for c in chunks]
    g_lasts = [g_alls[c][(N_LEVELS + 1) * CHUNK:(N_LEVELS + 2) * CHUNK] for c in chunks]

    a_mats = [masks_ref[0] * _bdot_nt(q[sls[c]], k[sls[c]]) for c in chunks]
    for lvl in range(1, N_LEVELS + 1):
        for c in chunks:
            g_ref = g_alls[c][lvl * CHUNK:(lvl + 1) * CHUNK]
            wgt = jnp.exp(-jnp.abs(gs[c] - g_ref))
            a_mats[c] = a_mats[c] + masks_ref[lvl] * _bdot_nt(q[sls[c]] * wgt, k[sls[c]] * wgt)

    incs = [_bdot_tn(v[sls[c]], k[sls[c]] * jnp.exp(g_lasts[c] - gs[c])) for c in chunks]
    intras = [_bdot(a_mats[c], v[sls[c]]) for c in chunks]

    state_t = state_ref[...]
    outs = []
    for c in chunks:
        outs.append(_bdot_nt(q[sls[c]] * jnp.exp(gs[c]), state_t) + intras[c])
        state_t = jnp.exp(g_lasts[c][0:1, :]) * state_t + incs[c]

    state_ref[...] = state_t
    o = jnp.concatenate(outs, axis=0)
    y_ref[...] = _gated_head_norm(o, z_ref[...], nw_ref[...]).astype(y_ref.dtype)


def _hgrn2(proj, lb_logits, norm_w, sums, masks, layer, col0, batch, seq, n_heads):
    tokens = proj.shape[0]
    depth = lb_logits.shape[0]
    steps = seq // HG_ROWS
    row_map = lambda c0: (lambda b, h, t: (b * steps + t, c0 + h))
    return pl.pallas_call(
        functools.partial(_hgrn2_kernel, layer),
        grid=(batch, n_heads, steps),
        in_specs=[
            pl.BlockSpec((HG_ROWS, HEAD_DIM), row_map(col0)),
            pl.BlockSpec((HG_ROWS, HEAD_DIM), row_map(col0 + n_heads)),
            pl.BlockSpec((HG_ROWS, HEAD_DIM), row_map(col0 + 2 * n_heads)),
            pl.BlockSpec((HG_ROWS, HEAD_DIM), row_map(col0 + 3 * n_heads)),
            pl.BlockSpec((depth, HEAD_DIM), lambda b, h, t: (0, h)),
            pl.BlockSpec((1, HEAD_DIM), lambda b, h, t: (0, 0)),
            pl.BlockSpec(sums.shape, lambda b, h, t: (0, 0)),
            pl.BlockSpec(masks.shape, lambda b, h, t: (0, 0, 0)),
        ],
        out_specs=pl.BlockSpec((HG_ROWS, HEAD_DIM), lambda b, h, t: (b * steps + t, h)),
        out_shape=jax.ShapeDtypeStruct((tokens, n_heads * HEAD_DIM), BF16),
        scratch_shapes=[pltpu.VMEM((HEAD_DIM, HEAD_DIM), F32)],
        compiler_params=pltpu.CompilerParams(
            dimension_semantics=("arbitrary", "arbitrary", "arbitrary"),
            vmem_limit_bytes=VMEM_LIMIT_BYTES),
        name="hgrn2",
    )(proj, proj, proj, proj, lb_logits, norm_w, sums, masks)


def _outproj_kernel(final, h_ref, ya_ref, yb_ref, p_ref, wo_ref, wg_ref, wu_ref, nw_ref, o_ref):
    y = jnp.concatenate([ya_ref[...], yb_ref[...]], axis=1)
    h1 = h_ref[...] + jnp.dot(y, wo_ref[...], preferred_element_type=F32)
    gate = _sigmoid(jnp.dot(h1.astype(BF16), wg_ref[...], preferred_element_type=F32))
    up = jnp.dot(p_ref[...].astype(BF16), wu_ref[...], preferred_element_type=F32)
    h2 = h1 + up * gate
    if final:
        ms = jnp.mean(h2 * h2, axis=-1, keepdims=True)
        h2 = h2 * lax.rsqrt(ms + NORM_EPS) * nw_ref[...]
    o_ref[...] = h2


def _outproj(h, ya, yb, p, layer, w_out, w_gate, w_up, final_nw, final):
    tokens, d_model = h.shape
    half = ya.shape[1]
    ple = p.shape[2]
    resident = lambda shape: pl.BlockSpec(shape, lambda i: (0, 0), pipeline_mode=pl.Buffered(1))
    return pl.pallas_call(
        functools.partial(_outproj_kernel, final),
        grid=(tokens // OUT_ROWS,),
        in_specs=[
            pl.BlockSpec((OUT_ROWS, d_model), lambda i: (i, 0)),
            pl.BlockSpec((OUT_ROWS, half), lambda i: (i, 0)),
            pl.BlockSpec((OUT_ROWS, half), lambda i: (i, 0)),
            pl.BlockSpec((None, OUT_ROWS, ple), lambda i: (layer, i, 0)),
            resident(w_out.shape),
            resident(w_gate.shape),
            resident(w_up.shape),
            resident(final_nw.shape),
        ],
        out_specs=pl.BlockSpec((OUT_ROWS, d_model), lambda i: (i, 0)),
        out_shape=jax.ShapeDtypeStruct((tokens, d_model), F32),
        compiler_params=pltpu.CompilerParams(
            dimension_semantics=("arbitrary",),
            vmem_limit_bytes=VMEM_LIMIT_BYTES),
        name="outproj",
    )(h, ya, yb, p, w_out, w_gate, w_up, final_nw)


def kernel(x, p, norm_w, w_in, dn_conv_w, dn_A_log, dn_dt_bias, dn_norm_w, hg_lb_logits, hg_norm_w,
           w_out, w_ple_up, w_ple_gate, final_norm_w):
    batch, seq, d_model = x.shape
    depth = w_in.shape[0]
    dn_heads = dn_A_log.shape[1]
    dn_width = dn_heads * HEAD_DIM
    hg_width = hg_lb_logits.shape[1]
    hg_heads = hg_width // HEAD_DIM
    tokens = batch * seq
    gate0 = 4 * dn_width
    gate1 = gate0 + 2 * dn_heads
    hg_col0 = gate0 // HEAD_DIM
    assert dn_heads % DN_GROUP == 0 and seq % DN_ROWS == 0 and seq % HG_ROWS == 0

    sums_np, masks_np = _level_constants()
    col_sel_np, row_cum_np = _pair_constants()
    sums = jnp.asarray(sums_np, dtype=BF16)
    masks = jnp.asarray(masks_np)
    masks_pair = jnp.asarray(np.concatenate([masks_np, masks_np], axis=2))
    col_sel = jnp.asarray(col_sel_np, dtype=BF16)
    row_cum = jnp.asarray(row_cum_np, dtype=BF16)

    h = x.reshape(tokens, d_model)
    for l in range(depth):
        wt_l = w_in[l].T
        proj, bat = _inproj(h, norm_w[l][None, :], wt_l[:gate0].astype(BF16), wt_l[gate1:].astype(BF16),
                            wt_l[gate0:gate1].astype(BF16))
        bat4 = bat.reshape(2, dn_heads // DN_GROUP, DN_GROUP, tokens)
        ya = _deltanet(proj, bat4, dn_conv_w[l], dn_A_log[l], dn_dt_bias[l], dn_norm_w[l][None, :],
                       masks_pair, col_sel, row_cum, batch, seq, dn_heads)
        yb = _hgrn2(proj, hg_lb_logits, hg_norm_w[l][None, :], sums, masks, l, hg_col0,
                    batch, seq, hg_heads)
        h = _outproj(h, ya, yb, p.reshape(depth, tokens, -1), l, w_out[l].astype(BF16),
                     w_ple_gate[l].astype(BF16), w_ple_up[l].astype(BF16), final_norm_w[None, :],
                     l == depth - 1)
    return h.reshape(batch, seq, d_model)
```

```python
import functools

import jax
import jax.numpy as jnp
import numpy as np
from jax import lax
from jax.experimental import pallas as pl
from jax.experimental.pallas import tpu as pltpu

F32 = jnp.float32
BF16 = jnp.bfloat16

HEAD_DIM = 128
CHUNK = 64
PAIR = 2 * CHUNK
CONV_WIDTH = 4
NORM_EPS = 1e-6
L2_EPS = 1e-6
N_LEVELS = 6
VMEM_LIMIT_BYTES = 56 * 1024 * 1024

IN_ROWS = 1024
IN_COLS = 1024
OUT_ROWS = 256
HG_CHUNKS = 16
HG_ROWS = HG_CHUNKS * CHUNK
DN_GROUP = 4
DN_ROWS = 4 * CHUNK


def _bdot(a, b):
    return jnp.dot(a.astype(BF16), b.astype(BF16), preferred_element_type=F32)


def _bdot_nt(a, b):
    return lax.dot_general(a.astype(BF16), b.astype(BF16), (((1,), (1,)), ((), ())),
                           preferred_element_type=F32)


def _bdot_tn(a, b):
    return lax.dot_general(a.astype(BF16), b.astype(BF16), (((0,), (0,)), ((), ())),
                           preferred_element_type=F32)


def _split2(x):
    hi = x.astype(BF16)
    lo = (x - hi.astype(F32)).astype(BF16)
    return hi, lo


def _sigmoid_pair(x):
    e = jnp.exp(-jnp.abs(x))
    r = 1.0 / (1.0 + e)
    er = e * r
    pos = x >= 0
    return jnp.where(pos, r, er), jnp.where(pos, er, r)


def _sigmoid(x):
    return 1.0 / (1.0 + jnp.exp(-x))


def _silu(x):
    return x * _sigmoid(x)


def _softplus(x):
    return jnp.maximum(x, 0.0) + jnp.log1p(jnp.exp(-jnp.abs(x)))


def _gated_head_norm(o, z, w):
    o = o * lax.rsqrt(jnp.mean(o * o, axis=-1, keepdims=True) + NORM_EPS) * w
    return o * _silu(z)


def _inproj_kernel(h_ref, nw_ref, w_ref, wbat_ref, proj_ref, bat_ref, hn_ref):
    contract_last = (((1,), (1,)), ((), ()))

    @pl.when(pl.program_id(1) == 0)
    def _():
        x = h_ref[...]
        ms = jnp.mean(x * x, axis=-1, keepdims=True)
        hn = (x * lax.rsqrt(ms + NORM_EPS) * nw_ref[...]).astype(BF16)
        hn_ref[...] = hn
        bat_ref[...] = lax.dot_general(wbat_ref[...], hn, contract_last, preferred_element_type=F32)

    proj_ref[...] = lax.dot_general(hn_ref[...], w_ref[...], contract_last, preferred_element_type=F32)


def _inproj(h, nw, w_t, gate0, n_gate):
    tokens, d_model = h.shape
    width = w_t.shape[0] - n_gate
    assert gate0 % IN_COLS == 0 and width % IN_COLS == 0
    skip_from = gate0 // IN_COLS
    grid = (tokens // IN_ROWS, width // IN_COLS)
    return pl.pallas_call(
        _inproj_kernel,
        grid=grid,
        in_specs=[
            pl.BlockSpec((IN_ROWS, d_model), lambda i, j: (i, 0)),
            pl.BlockSpec((1, d_model), lambda i, j: (0, 0)),
            pl.BlockSpec((pl.Element(IN_COLS), pl.Element(d_model)),
                         lambda i, j: ((j * (IN_COLS // n_gate) + (j >= skip_from).astype(jnp.int32)) * n_gate, 0)),
            pl.BlockSpec((pl.Element(n_gate), pl.Element(d_model)), lambda i, j: (gate0, 0)),
        ],
        out_specs=[
            pl.BlockSpec((IN_ROWS, IN_COLS), lambda i, j: (i, j)),
            pl.BlockSpec((n_gate, IN_ROWS), lambda i, j: (0, i)),
        ],
        out_shape=[
            jax.ShapeDtypeStruct((tokens, width), F32),
            jax.ShapeDtypeStruct((n_gate, tokens), F32),
        ],
        scratch_shapes=[pltpu.VMEM((IN_ROWS, d_model), BF16)],
        compiler_params=pltpu.CompilerParams(
            dimension_semantics=("arbitrary", "arbitrary"),
            vmem_limit_bytes=VMEM_LIMIT_BYTES),
        name="inproj",
    )(h, nw, w_t, w_t)


def _level_constants():
    t = np.arange(CHUNK)
    sums = [(t[None, :] <= t[:, None])]
    masks = [np.eye(CHUNK, dtype=bool)]
    for lvl in range(1, N_LEVELS + 1):
        size, half = 1 << lvl, 1 << (lvl - 1)
        ref = (t // size) * size + half - 1
        sums.append(t[None, :] <= ref[:, None])
        same = (t[:, None] // size) == (t[None, :] // size)
        upper_r = ((t // half) % 2 == 1)[:, None]
        lower_s = ((t // half) % 2 == 0)[None, :]
        masks.append(same & upper_r & lower_s)
    sums.append(np.ones((CHUNK, CHUNK), dtype=bool))
    sums = np.concatenate(sums, axis=0).astype(np.float32)
    sums = np.concatenate([sums, sums], axis=1)
    masks = np.stack(masks, axis=0).astype(np.float32)
    assert masks.sum(axis=0).tolist() == np.tril(np.ones((CHUNK, CHUNK))).tolist()
    return sums, masks


def _pair_constants():
    k = np.arange(2 * PAIR)
    n2 = np.arange(2 * PAIR)
    n1 = np.arange(PAIR)
    k_chunk = (k % PAIR) // CHUNK
    k_pos = k % CHUNK
    col_sel = (k_chunk[:, None] == (n2 // PAIR)[None, :])
    row_cum = (k_chunk[:, None] == (n1 // CHUNK)[None, :]) & (k_pos[:, None] <= (n1 % CHUNK)[None, :])
    return col_sel.astype(np.float32), row_cum.astype(np.float32)


def _conv_silu(x_ref, w_ref, buf_ref, lanes):
    rows = x_ref.shape[0]
    x = x_ref[:, lanes]
    buf_ref[pl.ds(8, rows), lanes] = x
    w = w_ref[:, lanes]
    acc = w[CONV_WIDTH - 1:CONV_WIDTH, :] * x
    for j in range(CONV_WIDTH - 1):
        shift = CONV_WIDTH - 1 - j
        acc = acc + w[j:j + 1, :] * buf_ref[pl.ds(8 - shift, rows), lanes]
    buf_ref[pl.ds(0, 8), lanes] = x[rows - 8:rows, :]
    return _silu(acc)


def _deltanet_kernel(alog_ref, dtb_ref, q_ref, k_ref, v_ref, z_ref, b_ref, a_ref,
                     wq_ref, wk_ref, wv_ref, nw_ref, masks_ref, colsel_ref, rowcum_ref, y_ref,
                     state_ref, qbuf_ref, kbuf_ref, vbuf_ref):
    group = pl.program_id(1)
    width = q_ref.shape[1]

    @pl.when(pl.program_id(2) == 0)
    def _():
        state_ref[...] = jnp.zeros_like(state_ref)
        qbuf_ref[pl.ds(0, 8), :] = jnp.zeros((8, width), F32)
        kbuf_ref[pl.ds(0, 8), :] = jnp.zeros((8, width), F32)
        vbuf_ref[pl.ds(0, 8), :] = jnp.zeros((8, width), F32)

    rows = q_ref.shape[0]
    n_pairs = rows // PAIR
    heads = range(DN_GROUP)
    units = [(j, p) for j in heads for p in range(n_pairs)]
    n_units = len(units)

    lane = lax.broadcasted_iota(jnp.int32, (CHUNK, PAIR), 1)
    row = lax.broadcasted_iota(jnp.int32, (CHUNK, PAIR), 0)
    col = lane & (CHUNK - 1)
    left = lane < CHUNK
    causal = col <= row
    strict = col < row
    eye = col == row
    left_b = jnp.where(left, 1.0, 0.0).astype(BF16)
    right_b = jnp.where(left, 0.0, 1.0).astype(BF16)
    level_b = [masks_ref[lvl].astype(BF16) for lvl in range(N_LEVELS + 1)]
    col_sel = colsel_ref[...]
    row_cum = rowcum_ref[...]

    def block_diag(m):
        return jnp.concatenate([m * left_b, m * right_b], axis=0)

    def pair_dot3(x_hi, x_lo, y_hi, y_lo):
        bd_hi = block_diag(y_hi)
        both = jnp.dot(x_hi, jnp.concatenate([bd_hi, block_diag(y_lo)], axis=1),
                       preferred_element_type=F32)
        return (both[:, 0:PAIR] + both[:, PAIR:2 * PAIR]
                + jnp.dot(x_lo, bd_hi, preferred_element_type=F32))

    q_h, k_h, v_h, beta_rows, g_rows = [], [], [], [], []
    for j in heads:
        sl = slice(j * HEAD_DIM, (j + 1) * HEAD_DIM)
        qj = _conv_silu(q_ref, wq_ref, qbuf_ref, sl)
        kj = _conv_silu(k_ref, wk_ref, kbuf_ref, sl)
        v_h.append(_conv_silu(v_ref, wv_ref, vbuf_ref, sl))
        q_h.append(qj * lax.rsqrt(jnp.sum(qj * qj, axis=-1, keepdims=True) + L2_EPS) * (HEAD_DIM ** -0.5))
        k_h.append(kj * lax.rsqrt(jnp.sum(kj * kj, axis=-1, keepdims=True) + L2_EPS))
        head = group * DN_GROUP + j
        beta_rows.append(_sigmoid(b_ref[j:j + 1, :]))
        rate = jnp.exp(jnp.full((1, rows), alog_ref[head], F32))
        g_rows.append(-rate * _softplus(a_ref[j:j + 1, :] + dtb_ref[head]))

    cols, g_rowcums = [], []
    for j, p in units:
        ps = slice(p * PAIR, (p + 1) * PAIR)
        g_b = jnp.broadcast_to(g_rows[j][:, ps], (CHUNK, PAIR))
        beta_b = jnp.broadcast_to(beta_rows[j][:, ps], (CHUNK, PAIR))
        lhs = jnp.concatenate([jnp.where(causal, g_b, 0.0), jnp.where(eye, beta_b, 0.0), g_b], axis=0)
        hi, lo = _split2(lhs)
        hilo = jnp.concatenate([hi, lo], axis=1)
        cols.append(jnp.dot(hilo, col_sel, preferred_element_type=F32))
        g_rowcums.append(jnp.dot(hilo[2 * CHUNK:3 * CHUNK], row_cum,
                                 preferred_element_type=F32))

    def stacked(u, block):
        c = cols[u][block * CHUNK:(block + 1) * CHUNK]
        return jnp.concatenate([c[:, 0:HEAD_DIM], c[:, HEAD_DIM:2 * HEAD_DIM]], axis=0)

    g_cols = [stacked(u, 0) for u in range(n_units)]
    beta_cols = [stacked(u, 1) for u in range(n_units)]
    g_lasts = [stacked(u, 2) for u in range(n_units)]
    exp_gs = [jnp.exp(g) for g in g_cols]
    q_u = [q_h[j][p * PAIR:(p + 1) * PAIR] for j, p in units]
    k_u = [k_h[j][p * PAIR:(p + 1) * PAIR] for j, p in units]
    v_u = [v_h[j][p * PAIR:(p + 1) * PAIR] for j, p in units]
    k_betas = [k_u[u] * beta_cols[u] for u in range(n_units)]

    prods = [_bdot_nt(jnp.concatenate([k_betas[u], q_u[u]], axis=0), k_u[u]) for u in range(n_units)]

    a_his, a_los, qks, invs = [], [], [], []
    for u in range(n_units):
        g_pair = jnp.where(left, g_cols[u][0:CHUNK], g_cols[u][CHUNK:PAIR])
        diff = g_pair - g_rowcums[u]
        decay = jnp.where(causal, jnp.exp(jnp.where(causal, diff, 0.0)), 0.0)
        pr = prods[u]
        kk = jnp.where(left, pr[0:CHUNK], pr[CHUNK:PAIR])
        qk = jnp.where(left, pr[PAIR:PAIR + CHUNK], pr[PAIR + CHUNK:2 * PAIR])
        a_mat = jnp.where(strict, kk * decay, 0.0)
        a_hi, a_lo = _split2(a_mat)
        a_his.append(a_hi)
        a_los.append(a_lo)
        qks.append(qk * decay)
        invs.append(jnp.where(eye, 1.0, 0.0) - masks_ref[1] * a_mat)

    for lvl in range(2, N_LEVELS + 1):
        t_split = [_split2(t) for t in invs]
        xs = [pair_dot3(a_his[u] * level_b[lvl], a_los[u] * level_b[lvl], *t_split[u])
              for u in range(n_units)]
        invs = [invs[u] - pair_dot3(*t_split[u], *_split2(xs[u])) for u in range(n_units)]

    sols, qk_sols, kt_sols = [], [], []
    for u in range(n_units):
        rhs = jnp.concatenate([v_u[u] * beta_cols[u], k_betas[u] * exp_gs[u]], axis=1)
        sols.append(jnp.dot(block_diag(invs[u].astype(BF16)), rhs.astype(BF16),
                            preferred_element_type=F32))
    for u in range(n_units):
        qk_sols.append(jnp.dot(block_diag(qks[u].astype(BF16)), sols[u].astype(BF16),
                               preferred_element_type=F32))
    for u in range(n_units):
        k_tail = k_u[u] * jnp.exp(g_lasts[u] - g_cols[u])
        kt_sols.append([_bdot_tn(k_tail[c * CHUNK:(c + 1) * CHUNK], sols[u][c * CHUNK:(c + 1) * CHUNK])
                        for c in range(2)])

    states = [state_ref[j] for j in heads]
    outs = [[] for _ in heads]
    for p in range(n_pairs):
        for c in range(2):
            cs = slice(c * CHUNK, (c + 1) * CHUNK)
            for j in heads:
                u = j * n_pairs + p
                kt_sol = kt_sols[u][c]
                lhs = jnp.concatenate(
                    [-kt_sol[:, HEAD_DIM:2 * HEAD_DIM],
                     q_u[u][cs] * exp_gs[u][cs] - qk_sols[u][cs, HEAD_DIM:2 * HEAD_DIM]], axis=0)
                both = _bdot(lhs, states[j])
                tail = jnp.exp(g_lasts[u][c * CHUNK:c * CHUNK + 1, :])
                states[j] = tail * states[j] + both[0:HEAD_DIM] + kt_sol[:, 0:HEAD_DIM]
                outs[j].append(both[HEAD_DIM:HEAD_DIM + CHUNK] + qk_sols[u][cs, 0:HEAD_DIM])

    for j in heads:
        state_ref[j] = states[j]
        sl = slice(j * HEAD_DIM, (j + 1) * HEAD_DIM)
        o = jnp.concatenate(outs[j], axis=0)
        y_ref[:, sl] = _gated_head_norm(o, z_ref[:, sl], nw_ref[...]).astype(y_ref.dtype)


def _deltanet(proj, bat4, conv_w, a_log, dt_bias, norm_w, masks_pair, col_sel, row_cum, batch, seq, n_heads):
    tokens = proj.shape[0]
    steps = seq // DN_ROWS
    groups = n_heads // DN_GROUP
    width = DN_GROUP * HEAD_DIM
    row_map = lambda col0: (lambda b, g, t: (b * steps + t, col0 + g))
    gate_map = lambda kind: (lambda b, g, t: (kind, g, 0, b * steps + t))
    conv_map = lambda col0: (lambda b, g, t: (0, col0 + g))
    const2 = lambda b, g, t: (0, 0)
    smem = pl.BlockSpec(memory_space=pltpu.SMEM)
    return pl.pallas_call(
        _deltanet_kernel,
        grid=(batch, groups, steps),
        in_specs=[
            smem, smem,
            pl.BlockSpec((DN_ROWS, width), row_map(0)),
            pl.BlockSpec((DN_ROWS, width), row_map(groups)),
            pl.BlockSpec((DN_ROWS, width), row_map(2 * groups)),
            pl.BlockSpec((DN_ROWS, width), row_map(3 * groups)),
            pl.BlockSpec((None, None, DN_GROUP, DN_ROWS), gate_map(0)),
            pl.BlockSpec((None, None, DN_GROUP, DN_ROWS), gate_map(1)),
            pl.BlockSpec((CONV_WIDTH, width), conv_map(0)),
            pl.BlockSpec((CONV_WIDTH, width), conv_map(groups)),
            pl.BlockSpec((CONV_WIDTH, width), conv_map(2 * groups)),
            pl.BlockSpec((1, HEAD_DIM), const2),
            pl.BlockSpec(masks_pair.shape, lambda b, g, t: (0, 0, 0)),
            pl.BlockSpec(col_sel.shape, const2),
            pl.BlockSpec(row_cum.shape, const2),
        ],
        out_specs=pl.BlockSpec((DN_ROWS, width), lambda b, g, t: (b * steps + t, g)),
        out_shape=jax.ShapeDtypeStruct((tokens, n_heads * HEAD_DIM), BF16),
        scratch_shapes=[
            pltpu.VMEM((DN_GROUP, HEAD_DIM, HEAD_DIM), F32),
            pltpu.VMEM((DN_ROWS + 8, width), F32),
            pltpu.VMEM((DN_ROWS + 8, width), F32),
            pltpu.VMEM((DN_ROWS + 8, width), F32),
        ],
        compiler_params=pltpu.CompilerParams(
            dimension_semantics=("arbitrary", "arbitrary", "arbitrary"),
            vmem_limit_bytes=VMEM_LIMIT_BYTES),
        name="deltanet",
    )(a_log, dt_bias, proj, proj, proj, proj, bat4, bat4, conv_w, conv_w, conv_w, norm_w,
      masks_pair, col_sel, row_cum)


def _hgrn2_kernel(layer, q_ref, f_ref, i_ref, z_ref, lbl_ref, nw_ref, sums_ref, masks_ref, y_ref,
                  state_ref):
    @pl.when(pl.program_id(2) == 0)
    def _():
        state_ref[...] = jnp.zeros_like(state_ref)

    logits = lbl_ref[...]
    ex = jnp.exp(logits - jnp.max(logits, axis=0, keepdims=True))
    probs = ex / jnp.sum(ex, axis=0, keepdims=True)
    lb = jnp.sum(probs[0:layer + 1], axis=0, keepdims=True) - probs[0:1]

    sig, nsig = _sigmoid_pair(f_ref[...])
    log_f = jnp.log(lb + (1.0 - lb) * sig)
    k = (1.0 - lb) * nsig
    q = _silu(q_ref[...])
    v = i_ref[...]

    sums2 = sums_ref[...]

    rows = q.shape[0]
    chunks = range(rows // CHUNK)
    sls = [slice(c * CHUNK, (c + 1) * CHUNK) for c in chunks]

    def cumulative(c):
        hi, lo = _split2(log_f[sls[c]])
        return jnp.dot(sums2, jnp.concatenate([hi, lo], axis=0), preferred_element_type=F32)

    def intra_matrix(c, g_all):
        g = g_all[0:CHUNK]
        a_mat = masks_ref[0] * _bdot_nt(q[sls[c]], k[sls[c]])
        for lvl in range(1, N_LEVELS + 1):
            wgt = jnp.exp(-jnp.abs(g - g_all[lvl * CHUNK:(lvl + 1) * CHUNK]))
            a_mat = a_mat + masks_ref[lvl] * _bdot_nt(q[sls[c]] * wgt, k[sls[c]] * wgt)
        return a_mat

    LOOKAHEAD = 2
    n_chunks = len(chunks)
    g_alls = {c: cumulative(c) for c in range(min(LOOKAHEAD, n_chunks))}
    gs, g_lasts, a_mats, incs, intras = [], [], [], [], []
    for c in chunks:
        g_all = g_alls.pop(c)
        gs.append(g_all[0:CHUNK])
        g_lasts.append(g_all[(N_LEVELS + 1) * CHUNK:(N_LEVELS + 2) * CHUNK])
        a_mats.append(intra_matrix(c, g_all))
        if c + LOOKAHEAD < n_chunks:
            g_alls[c + LOOKAHEAD] = cumulative(c + LOOKAHEAD)
        incs.append(_bdot_tn(v[sls[c]], k[sls[c]] * jnp.exp(g_lasts[c] - gs[c])))
        if c >= 1:
            intras.append(_bdot(a_mats[c - 1], v[sls[c - 1]]))
    intras.append(_bdot(a_mats[n_chunks - 1], v[sls[n_chunks - 1]]))

    state_t = state_ref[...]
    outs = []
    for c in chunks:
        outs.append(_bdot_nt(q[sls[c]] * jnp.exp(gs[c]), state_t) + intras[c])
        state_t = jnp.exp(g_lasts[c][0:1, :]) * state_t + incs[c]

    state_ref[...] = state_t
    o = jnp.concatenate(outs, axis=0)
    y_ref[...] = _gated_head_norm(o, z_ref[...], nw_ref[...]).astype(y_ref.dtype)


def _hgrn2(proj, lb_logits, norm_w, sums, masks, layer, col0, batch, seq, n_heads):
    tokens = proj.shape[0]
    depth = lb_logits.shape[0]
    steps = seq // HG_ROWS
    row_map = lambda c0: (lambda b, h, t: (b * steps + t, c0 + h))
    return pl.pallas_call(
        functools.partial(_hgrn2_kernel, layer),
        grid=(batch, n_heads, steps),
        in_specs=[
            pl.BlockSpec((HG_ROWS, HEAD_DIM), row_map(col0)),
            pl.BlockSpec((HG_ROWS, HEAD_DIM), row_map(col0 + n_heads)),
            pl.BlockSpec((HG_ROWS, HEAD_DIM), row_map(col0 + 2 * n_heads)),
            pl.BlockSpec((HG_ROWS, HEAD_DIM), row_map(col0 + 3 * n_heads)),
            pl.BlockSpec((depth, HEAD_DIM), lambda b, h, t: (0, h)),
            pl.BlockSpec((1, HEAD_DIM), lambda b, h, t: (0, 0)),
            pl.BlockSpec(sums.shape, lambda b, h, t: (0, 0)),
            pl.BlockSpec(masks.shape, lambda b, h, t: (0, 0, 0)),
        ],
        out_specs=pl.BlockSpec((HG_ROWS, HEAD_DIM), lambda b, h, t: (b * steps + t, h)),
        out_shape=jax.ShapeDtypeStruct((tokens, n_heads * HEAD_DIM), BF16),
        scratch_shapes=[pltpu.VMEM((HEAD_DIM, HEAD_DIM), F32)],
        compiler_params=pltpu.CompilerParams(
            dimension_semantics=("arbitrary", "arbitrary", "arbitrary"),
            vmem_limit_bytes=VMEM_LIMIT_BYTES),
        name="hgrn2",
    )(proj, proj, proj, proj, lb_logits, norm_w, sums, masks)


def _outproj_kernel(final, h_ref, ya_ref, yb_ref, p_ref, wo_ref, wg_ref, wu_ref, nw_ref, o_ref):
    y = jnp.concatenate([ya_ref[...], yb_ref[...]], axis=1)
    h1 = h_ref[...] + jnp.dot(y, wo_ref[...], preferred_element_type=F32)
    gate = _sigmoid(jnp.dot(h1.astype(BF16), wg_ref[...], preferred_element_type=F32))
    up = jnp.dot(p_ref[...].astype(BF16), wu_ref[...], preferred_element_type=F32)
    h2 = h1 + up * gate
    if final:
        ms = jnp.mean(h2 * h2, axis=-1, keepdims=True)
        h2 = h2 * lax.rsqrt(ms + NORM_EPS) * nw_ref[...]
    o_ref[...] = h2


def _outproj(h, ya, yb, p, layer, w_out, w_gate, w_up, final_nw, final):
    tokens, d_model = h.shape
    half = ya.shape[1]
    ple = p.shape[2]
    resident = lambda shape: pl.BlockSpec(shape, lambda i: (0, 0), pipeline_mode=pl.Buffered(1))
    return pl.pallas_call(
        functools.partial(_outproj_kernel, final),
        grid=(tokens // OUT_ROWS,),
        in_specs=[
            pl.BlockSpec((OUT_ROWS, d_model), lambda i: (i, 0)),
            pl.BlockSpec((OUT_ROWS, half), lambda i: (i, 0)),
            pl.BlockSpec((OUT_ROWS, half), lambda i: (i, 0)),
            pl.BlockSpec((None, OUT_ROWS, ple), lambda i: (layer, i, 0)),
            resident(w_out.shape),
            resident(w_gate.shape),
            resident(w_up.shape),
            resident(final_nw.shape),
        ],
        out_specs=pl.BlockSpec((OUT_ROWS, d_model), lambda i: (i, 0)),
        out_shape=jax.ShapeDtypeStruct((tokens, d_model), F32),
        compiler_params=pltpu.CompilerParams(
            dimension_semantics=("arbitrary",),
            vmem_limit_bytes=VMEM_LIMIT_BYTES),
        name="outproj",
    )(h, ya, yb, p, w_out, w_gate, w_up, final_nw)


def kernel(x, p, norm_w, w_in, dn_conv_w, dn_A_log, dn_dt_bias, dn_norm_w, hg_lb_logits, hg_norm_w,
           w_out, w_ple_up, w_ple_gate, final_norm_w):
    batch, seq, d_model = x.shape
    depth = w_in.shape[0]
    dn_heads = dn_A_log.shape[1]
    dn_width = dn_heads * HEAD_DIM
    hg_width = hg_lb_logits.shape[1]
    hg_heads = hg_width // HEAD_DIM
    tokens = batch * seq
    gate0 = 4 * dn_width
    gate1 = gate0 + 2 * dn_heads
    hg_col0 = gate0 // HEAD_DIM
    assert dn_heads % DN_GROUP == 0 and seq % DN_ROWS == 0 and seq % HG_ROWS == 0

    sums_np, masks_np = _level_constants()
    col_sel_np, row_cum_np = _pair_constants()
    sums = jnp.asarray(sums_np, dtype=BF16)
    masks = jnp.asarray(masks_np)
    masks_pair = jnp.asarray(np.concatenate([masks_np, masks_np], axis=2))
    col_sel = jnp.asarray(col_sel_np, dtype=BF16)
    row_cum = jnp.asarray(row_cum_np, dtype=BF16)

    h = x.reshape(tokens, d_model)
    for l in range(depth):
        proj, bat = _inproj(h, norm_w[l][None, :], w_in[l].T.astype(BF16), gate0, gate1 - gate0)
        bat4 = bat.reshape(2, dn_heads // DN_GROUP, DN_GROUP, tokens)
        ya = _deltanet(proj, bat4, dn_conv_w[l], dn_A_log[l], dn_dt_bias[l], dn_norm_w[l][None, :],
                       masks_pair, col_sel, row_cum, batch, seq, dn_heads)
        yb = _hgrn2(proj, hg_lb_logits, hg_norm_w[l][None, :], sums, masks, l, hg_col0,
                    batch, seq, hg_heads)
        h = _outproj(h, ya, yb, p.reshape(depth, tokens, -1), l, w_out[l].astype(BF16),
                     w_ple_gate[l].astype(BF16), w_ple_up[l].astype(BF16), final_norm_w[None, :],
                     l == depth - 1)
    return h.reshape(batch, seq, d_model)
```

```python
import functools

import jax
import jax.numpy as jnp
import numpy as np
from jax import lax
from jax.experimental import pallas as pl
from jax.experimental.pallas import tpu as pltpu

F32 = jnp.float32
BF16 = jnp.bfloat16

HEAD_DIM = 128
CHUNK = 64
PAIR = 2 * CHUNK
CONV_WIDTH = 4
NORM_EPS = 1e-6
L2_EPS = 1e-6
N_LEVELS = 6
VMEM_LIMIT_BYTES = 56 * 1024 * 1024

IN_ROWS = 1024
IN_COLS = 1024
OUT_ROWS = 256
HG_CHUNKS = 16
HG_ROWS = HG_CHUNKS * CHUNK
DN_GROUP = 4
DN_ROWS = 4 * CHUNK


def _bdot(a, b):
    return jnp.dot(a.astype(BF16), b.astype(BF16), preferred_element_type=F32)


def _bdot_nt(a, b):
    return lax.dot_general(a.astype(BF16), b.astype(BF16), (((1,), (1,)), ((), ())),
                           preferred_element_type=F32)


def _bdot_tn(a, b):
    return lax.dot_general(a.astype(BF16), b.astype(BF16), (((0,), (0,)), ((), ())),
                           preferred_element_type=F32)


def _split2(x):
    hi = x.astype(BF16)
    lo = (x - hi.astype(F32)).astype(BF16)
    return hi, lo


def _sigmoid_pair(x):
    e = jnp.exp(-jnp.abs(x))
    r = 1.0 / (1.0 + e)
    er = e * r
    pos = x >= 0
    return jnp.where(pos, r, er), jnp.where(pos, er, r)


def _sigmoid(x):
    return 1.0 / (1.0 + jnp.exp(-x))


def _silu(x):
    return x * _sigmoid(x)


def _softplus(x):
    return jnp.maximum(x, 0.0) + jnp.log1p(jnp.exp(-jnp.abs(x)))


def _gated_head_norm(o, z, w):
    o = o * lax.rsqrt(jnp.mean(o * o, axis=-1, keepdims=True) + NORM_EPS) * w
    return o * _silu(z)


def _inproj_kernel(h_ref, nw_ref, w_ref, wbat_ref, proj_ref, bat_ref, hn_ref):
    contract_last = (((1,), (1,)), ((), ()))

    @pl.when(pl.program_id(1) == 0)
    def _():
        x = h_ref[...]
        ms = jnp.mean(x * x, axis=-1, keepdims=True)
        hn = (x * lax.rsqrt(ms + NORM_EPS) * nw_ref[...]).astype(BF16)
        hn_ref[...] = hn
        bat_ref[...] = lax.dot_general(wbat_ref[0], hn, contract_last, preferred_element_type=F32)

    proj_ref[...] = lax.dot_general(hn_ref[...], w_ref[0], contract_last, preferred_element_type=F32)


def _inproj(h, nw, w_t, layer, gate0, n_gate):
    tokens, d_model = h.shape
    width = w_t.shape[1] - n_gate
    assert gate0 % IN_COLS == 0 and width % IN_COLS == 0
    skip_from = gate0 // IN_COLS
    grid = (tokens // IN_ROWS, width // IN_COLS)
    return pl.pallas_call(
        _inproj_kernel,
        grid=grid,
        in_specs=[
            pl.BlockSpec((IN_ROWS, d_model), lambda i, j: (i, 0)),
            pl.BlockSpec((1, d_model), lambda i, j: (0, 0)),
            pl.BlockSpec((pl.Element(1), pl.Element(IN_COLS), pl.Element(d_model)),
                         lambda i, j: (layer, (j * (IN_COLS // n_gate) + (j >= skip_from).astype(jnp.int32))
                                       * n_gate, 0)),
            pl.BlockSpec((pl.Element(1), pl.Element(n_gate), pl.Element(d_model)),
                         lambda i, j: (layer, gate0, 0)),
        ],
        out_specs=[
            pl.BlockSpec((IN_ROWS, IN_COLS), lambda i, j: (i, j)),
            pl.BlockSpec((n_gate, IN_ROWS), lambda i, j: (0, i)),
        ],
        out_shape=[
            jax.ShapeDtypeStruct((tokens, width), F32),
            jax.ShapeDtypeStruct((n_gate, tokens), F32),
        ],
        scratch_shapes=[pltpu.VMEM((IN_ROWS, d_model), BF16)],
        compiler_params=pltpu.CompilerParams(
            dimension_semantics=("arbitrary", "arbitrary"),
            vmem_limit_bytes=VMEM_LIMIT_BYTES),
        name="inproj",
    )(h, nw, w_t, w_t)


def _level_constants():
    t = np.arange(CHUNK)
    sums = [(t[None, :] <= t[:, None])]
    masks = [np.eye(CHUNK, dtype=bool)]
    for lvl in range(1, N_LEVELS + 1):
        size, half = 1 << lvl, 1 << (lvl - 1)
        ref = (t // size) * size + half - 1
        sums.append(t[None, :] <= ref[:, None])
        same = (t[:, None] // size) == (t[None, :] // size)
        upper_r = ((t // half) % 2 == 1)[:, None]
        lower_s = ((t // half) % 2 == 0)[None, :]
        masks.append(same & upper_r & lower_s)
    sums.append(np.ones((CHUNK, CHUNK), dtype=bool))
    sums = np.concatenate(sums, axis=0).astype(np.float32)
    sums = np.concatenate([sums, sums], axis=1)
    masks = np.stack(masks, axis=0).astype(np.float32)
    assert masks.sum(axis=0).tolist() == np.tril(np.ones((CHUNK, CHUNK))).tolist()
    return sums, masks


def _pair_constants():
    k = np.arange(2 * PAIR)
    n2 = np.arange(2 * PAIR)
    n1 = np.arange(PAIR)
    k_chunk = (k % PAIR) // CHUNK
    k_pos = k % CHUNK
    col_sel = (k_chunk[:, None] == (n2 // PAIR)[None, :])
    row_cum = (k_chunk[:, None] == (n1 // CHUNK)[None, :]) & (k_pos[:, None] <= (n1 % CHUNK)[None, :])
    return col_sel.astype(np.float32), row_cum.astype(np.float32)


def _conv_silu(x_ref, w_ref, buf_ref, lanes):
    assert CONV_WIDTH == 4
    rows = x_ref.shape[0]
    x = x_ref[:, lanes]
    xe = jnp.concatenate([buf_ref[pl.ds(0, 8), lanes], x], axis=0)
    buf_ref[pl.ds(0, 8), lanes] = x[rows - 8:rows, :]
    w = w_ref[:, lanes]
    x1 = pltpu.roll(xe, 1, axis=0)
    near = w[3:4, :] * xe + w[2:3, :] * x1
    far = w[1:2, :] * xe + w[0:1, :] * x1
    acc = near + pltpu.roll(far, 2, axis=0)
    return _silu(acc[8:, :])


def _deltanet_kernel(alog_ref, dtb_ref, q_ref, k_ref, v_ref, z_ref, b_ref, a_ref,
                     wq_ref, wk_ref, wv_ref, nw_ref, masks_ref, colsel_ref, rowcum_ref, y_ref,
                     state_ref, qbuf_ref, kbuf_ref, vbuf_ref):
    group = pl.program_id(1)
    width = q_ref.shape[1]

    @pl.when(pl.program_id(2) == 0)
    def _():
        state_ref[...] = jnp.zeros_like(state_ref)
        qbuf_ref[pl.ds(0, 8), :] = jnp.zeros((8, width), F32)
        kbuf_ref[pl.ds(0, 8), :] = jnp.zeros((8, width), F32)
        vbuf_ref[pl.ds(0, 8), :] = jnp.zeros((8, width), F32)

    rows = q_ref.shape[0]
    n_pairs = rows // PAIR
    heads = range(DN_GROUP)
    units = [(j, p) for j in heads for p in range(n_pairs)]
    n_units = len(units)

    lane = lax.broadcasted_iota(jnp.int32, (CHUNK, PAIR), 1)
    row = lax.broadcasted_iota(jnp.int32, (CHUNK, PAIR), 0)
    col = lane & (CHUNK - 1)
    left = lane < CHUNK
    causal = col <= row
    strict = col < row
    eye = col == row
    left_b = jnp.where(left, 1.0, 0.0).astype(BF16)
    right_b = jnp.where(left, 0.0, 1.0).astype(BF16)
    level_b = [masks_ref[lvl].astype(BF16) for lvl in range(N_LEVELS + 1)]
    col_sel = colsel_ref[...]
    row_cum = rowcum_ref[...]

    def block_diag(m):
        return jnp.concatenate([m * left_b, m * right_b], axis=0)

    def pair_dot3(x_hi, x_lo, y_hi, y_lo):
        bd_hi = block_diag(y_hi)
        both = jnp.dot(x_hi, jnp.concatenate([bd_hi, block_diag(y_lo)], axis=1),
                       preferred_element_type=F32)
        return (both[:, 0:PAIR] + both[:, PAIR:2 * PAIR]
                + jnp.dot(x_lo, bd_hi, preferred_element_type=F32))

    q_h, k_h, v_h, beta_rows, g_rows = [], [], [], [], []
    for j in heads:
        sl = slice(j * HEAD_DIM, (j + 1) * HEAD_DIM)
        qj = _conv_silu(q_ref, wq_ref, qbuf_ref, sl)
        kj = _conv_silu(k_ref, wk_ref, kbuf_ref, sl)
        v_h.append(_conv_silu(v_ref, wv_ref, vbuf_ref, sl))
        q_h.append(qj * lax.rsqrt(jnp.sum(qj * qj, axis=-1, keepdims=True) + L2_EPS) * (HEAD_DIM ** -0.5))
        k_h.append(kj * lax.rsqrt(jnp.sum(kj * kj, axis=-1, keepdims=True) + L2_EPS))
        head = group * DN_GROUP + j
        beta_rows.append(_sigmoid(b_ref[j:j + 1, :]))
        rate = jnp.exp(jnp.full((1, rows), alog_ref[head], F32))
        g_rows.append(-rate * _softplus(a_ref[j:j + 1, :] + dtb_ref[head]))

    cols, g_rowcums = [], []
    for j, p in units:
        ps = slice(p * PAIR, (p + 1) * PAIR)
        g_b = jnp.broadcast_to(g_rows[j][:, ps], (CHUNK, PAIR))
        beta_b = jnp.broadcast_to(beta_rows[j][:, ps], (CHUNK, PAIR))
        lhs = jnp.concatenate([jnp.where(causal, g_b, 0.0), jnp.where(eye, beta_b, 0.0), g_b], axis=0)
        hi, lo = _split2(lhs)
        hilo = jnp.concatenate([hi, lo], axis=1)
        cols.append(jnp.dot(hilo, col_sel, preferred_element_type=F32))
        g_rowcums.append(jnp.dot(hilo[2 * CHUNK:3 * CHUNK], row_cum,
                                 preferred_element_type=F32))

    def stacked(u, block):
        c = cols[u][block * CHUNK:(block + 1) * CHUNK]
        return jnp.concatenate([c[:, 0:HEAD_DIM], c[:, HEAD_DIM:2 * HEAD_DIM]], axis=0)

    g_cols = [stacked(u, 0) for u in range(n_units)]
    beta_cols = [stacked(u, 1) for u in range(n_units)]
    g_lasts = [stacked(u, 2) for u in range(n_units)]
    exp_gs = [jnp.exp(g) for g in g_cols]
    q_u = [q_h[j][p * PAIR:(p + 1) * PAIR] for j, p in units]
    k_u = [k_h[j][p * PAIR:(p + 1) * PAIR] for j, p in units]
    v_u = [v_h[j][p * PAIR:(p + 1) * PAIR] for j, p in units]
    k_betas = [k_u[u] * beta_cols[u] for u in range(n_units)]

    prods = [_bdot_nt(jnp.concatenate([k_betas[u], q_u[u]], axis=0), k_u[u]) for u in range(n_units)]

    a_his, a_los, qks, invs = [], [], [], []
    for u in range(n_units):
        g_pair = jnp.where(left, g_cols[u][0:CHUNK], g_cols[u][CHUNK:PAIR])
        diff = g_pair - g_rowcums[u]
        decay = jnp.where(causal, jnp.exp(jnp.where(causal, diff, 0.0)), 0.0)
        pr = prods[u]
        kk = jnp.where(left, pr[0:CHUNK], pr[CHUNK:PAIR])
        qk = jnp.where(left, pr[PAIR:PAIR + CHUNK], pr[PAIR + CHUNK:2 * PAIR])
        a_mat = jnp.where(strict, kk * decay, 0.0)
        a_hi, a_lo = _split2(a_mat)
        a_his.append(a_hi)
        a_los.append(a_lo)
        qks.append(qk * decay)
        invs.append(jnp.where(eye, 1.0, 0.0) - masks_ref[1] * a_mat)

    for lvl in range(2, N_LEVELS + 1):
        t_split = [_split2(t) for t in invs]
        xs = [pair_dot3(a_his[u] * level_b[lvl], a_los[u] * level_b[lvl], *t_split[u])
              for u in range(n_units)]
        invs = [invs[u] - pair_dot3(*t_split[u], *_split2(xs[u])) for u in range(n_units)]

    sols, qk_sols, kt_sols = [], [], []
    for u in range(n_units):
        rhs = jnp.concatenate([v_u[u] * beta_cols[u], k_betas[u] * exp_gs[u]], axis=1)
        sols.append(jnp.dot(block_diag(invs[u].astype(BF16)), rhs.astype(BF16),
                            preferred_element_type=F32))
    for u in range(n_units):
        qk_sols.append(jnp.dot(block_diag(qks[u].astype(BF16)), sols[u].astype(BF16),
                               preferred_element_type=F32))
    for u in range(n_units):
        k_tail = k_u[u] * jnp.exp(g_lasts[u] - g_cols[u])
        kt_sols.append([_bdot_tn(k_tail[c * CHUNK:(c + 1) * CHUNK], sols[u][c * CHUNK:(c + 1) * CHUNK])
                        for c in range(2)])

    states = [state_ref[j] for j in heads]
    outs = [[] for _ in heads]
    for p in range(n_pairs):
        for c in range(2):
            cs = slice(c * CHUNK, (c + 1) * CHUNK)
            for j in heads:
                u = j * n_pairs + p
                kt_sol = kt_sols[u][c]
                lhs = jnp.concatenate(
                    [-kt_sol[:, HEAD_DIM:2 * HEAD_DIM],
                     q_u[u][cs] * exp_gs[u][cs] - qk_sols[u][cs, HEAD_DIM:2 * HEAD_DIM]], axis=0)
                both = _bdot(lhs, states[j])
                tail = jnp.exp(g_lasts[u][c * CHUNK:c * CHUNK + 1, :])
                states[j] = tail * states[j] + both[0:HEAD_DIM] + kt_sol[:, 0:HEAD_DIM]
                outs[j].append(both[HEAD_DIM:HEAD_DIM + CHUNK] + qk_sols[u][cs, 0:HEAD_DIM])

    for j in heads:
        state_ref[j] = states[j]
        sl = slice(j * HEAD_DIM, (j + 1) * HEAD_DIM)
        o = jnp.concatenate(outs[j], axis=0)
        y_ref[:, sl] = _gated_head_norm(o, z_ref[:, sl], nw_ref[...]).astype(y_ref.dtype)


def _deltanet(proj, bat4, conv_w, a_log, dt_bias, norm_w, masks_pair, col_sel, row_cum, batch, seq, n_heads):
    tokens = proj.shape[0]
    steps = seq // DN_ROWS
    groups = n_heads // DN_GROUP
    width = DN_GROUP * HEAD_DIM
    row_map = lambda col0: (lambda b, g, t: (b * steps + t, col0 + g))
    gate_map = lambda kind: (lambda b, g, t: (kind, g, 0, b * steps + t))
    conv_map = lambda col0: (lambda b, g, t: (0, col0 + g))
    const2 = lambda b, g, t: (0, 0)
    smem = pl.BlockSpec(memory_space=pltpu.SMEM)
    return pl.pallas_call(
        _deltanet_kernel,
        grid=(batch, groups, steps),
        in_specs=[
            smem, smem,
            pl.BlockSpec((DN_ROWS, width), row_map(0)),
            pl.BlockSpec((DN_ROWS, width), row_map(groups)),
            pl.BlockSpec((DN_ROWS, width), row_map(2 * groups)),
            pl.BlockSpec((DN_ROWS, width), row_map(3 * groups)),
            pl.BlockSpec((None, None, DN_GROUP, DN_ROWS), gate_map(0)),
            pl.BlockSpec((None, None, DN_GROUP, DN_ROWS), gate_map(1)),
            pl.BlockSpec((CONV_WIDTH, width), conv_map(0)),
            pl.BlockSpec((CONV_WIDTH, width), conv_map(groups)),
            pl.BlockSpec((CONV_WIDTH, width), conv_map(2 * groups)),
            pl.BlockSpec((1, HEAD_DIM), const2),
            pl.BlockSpec(masks_pair.shape, lambda b, g, t: (0, 0, 0)),
            pl.BlockSpec(col_sel.shape, const2),
            pl.BlockSpec(row_cum.shape, const2),
        ],
        out_specs=pl.BlockSpec((DN_ROWS, width), lambda b, g, t: (b * steps + t, g)),
        out_shape=jax.ShapeDtypeStruct((tokens, n_heads * HEAD_DIM), BF16),
        scratch_shapes=[
            pltpu.VMEM((DN_GROUP, HEAD_DIM, HEAD_DIM), F32),
            pltpu.VMEM((8, width), F32),
            pltpu.VMEM((8, width), F32),
            pltpu.VMEM((8, width), F32),
        ],
        compiler_params=pltpu.CompilerParams(
            dimension_semantics=("arbitrary", "arbitrary", "arbitrary"),
            vmem_limit_bytes=VMEM_LIMIT_BYTES),
        name="deltanet",
    )(a_log, dt_bias, proj, proj, proj, proj, bat4, bat4, conv_w, conv_w, conv_w, norm_w,
      masks_pair, col_sel, row_cum)


def _hgrn2_kernel(layer, q_ref, f_ref, i_ref, z_ref, lbl_ref, nw_ref, sums_ref, masks_ref, y_ref,
                  state_ref):
    @pl.when(pl.program_id(2) == 0)
    def _():
        state_ref[...] = jnp.zeros_like(state_ref)

    logits = lbl_ref[...]
    ex = jnp.exp(logits - jnp.max(logits, axis=0, keepdims=True))
    probs = ex / jnp.sum(ex, axis=0, keepdims=True)
    lb = jnp.sum(probs[0:layer + 1], axis=0, keepdims=True) - probs[0:1]

    sig, nsig = _sigmoid_pair(f_ref[...])
    log_f = jnp.log(lb + (1.0 - lb) * sig)
    k = (1.0 - lb) * nsig
    q = _silu(q_ref[...])
    v = i_ref[...]

    sums2 = sums_ref[...]

    rows = q.shape[0]
    chunks = range(rows // CHUNK)
    sls = [slice(c * CHUNK, (c + 1) * CHUNK) for c in chunks]

    def cumulative(c):
        hi, lo = _split2(log_f[sls[c]])
        return jnp.dot(sums2, jnp.concatenate([hi, lo], axis=0), preferred_element_type=F32)

    def intra_matrix(c, g_all):
        g = g_all[0:CHUNK]
        q_b, k_b = q[sls[c]].astype(BF16), k[sls[c]].astype(BF16)
        a_mat = masks_ref[0] * _bdot_nt(q_b, k_b)
        for lvl in range(1, N_LEVELS + 1):
            wgt = jnp.exp(-jnp.abs(g - g_all[lvl * CHUNK:(lvl + 1) * CHUNK])).astype(BF16)
            a_mat = a_mat + masks_ref[lvl] * _bdot_nt(q_b * wgt, k_b * wgt)
        return a_mat

    LOOKAHEAD = 2
    n_chunks = len(chunks)
    g_alls = {c: cumulative(c) for c in range(min(LOOKAHEAD, n_chunks))}
    gs, g_lasts, a_mats, incs, intras = [], [], [], [], []
    for c in chunks:
        g_all = g_alls.pop(c)
        gs.append(g_all[0:CHUNK])
        g_lasts.append(g_all[(N_LEVELS + 1) * CHUNK:(N_LEVELS + 2) * CHUNK])
        a_mats.append(intra_matrix(c, g_all))
        if c + LOOKAHEAD < n_chunks:
            g_alls[c + LOOKAHEAD] = cumulative(c + LOOKAHEAD)
        incs.append(_bdot_tn(v[sls[c]], k[sls[c]] * jnp.exp(g_lasts[c] - gs[c])))
        if c >= 1:
            intras.append(_bdot(a_mats[c - 1], v[sls[c - 1]]))
    intras.append(_bdot(a_mats[n_chunks - 1], v[sls[n_chunks - 1]]))

    state_t = state_ref[...]
    outs = []
    for c in chunks:
        outs.append(_bdot_nt(q[sls[c]] * jnp.exp(gs[c]), state_t) + intras[c])
        state_t = jnp.exp(g_lasts[c][0:1, :]) * state_t + incs[c]

    state_ref[...] = state_t
    o = jnp.concatenate(outs, axis=0)
    y_ref[...] = _gated_head_norm(o, z_ref[...], nw_ref[...]).astype(y_ref.dtype)


def _hgrn2(proj, lb_logits, norm_w, sums, masks, layer, col0, batch, seq, n_heads):
    tokens = proj.shape[0]
    depth = lb_logits.shape[0]
    steps = seq // HG_ROWS
    row_map = lambda c0: (lambda b, h, t: (b * steps + t, c0 + h))
    return pl.pallas_call(
        functools.partial(_hgrn2_kernel, layer),
        grid=(batch, n_heads, steps),
        in_specs=[
            pl.BlockSpec((HG_ROWS, HEAD_DIM), row_map(col0)),
            pl.BlockSpec((HG_ROWS, HEAD_DIM), row_map(col0 + n_heads)),
            pl.BlockSpec((HG_ROWS, HEAD_DIM), row_map(col0 + 2 * n_heads)),
            pl.BlockSpec((HG_ROWS, HEAD_DIM), row_map(col0 + 3 * n_heads)),
            pl.BlockSpec((depth, HEAD_DIM), lambda b, h, t: (0, h)),
            pl.BlockSpec((1, HEAD_DIM), lambda b, h, t: (0, 0)),
            pl.BlockSpec(sums.shape, lambda b, h, t: (0, 0)),
            pl.BlockSpec(masks.shape, lambda b, h, t: (0, 0, 0)),
        ],
        out_specs=pl.BlockSpec((HG_ROWS, HEAD_DIM), lambda b, h, t: (b * steps + t, h)),
        out_shape=jax.ShapeDtypeStruct((tokens, n_heads * HEAD_DIM), BF16),
        scratch_shapes=[pltpu.VMEM((HEAD_DIM, HEAD_DIM), F32)],
        compiler_params=pltpu.CompilerParams(
            dimension_semantics=("arbitrary", "arbitrary", "arbitrary"),
            vmem_limit_bytes=VMEM_LIMIT_BYTES),
        name="hgrn2",
    )(proj, proj, proj, proj, lb_logits, norm_w, sums, masks)


def _outproj_kernel(final, h_ref, ya_ref, yb_ref, p_ref, wo_ref, wg_ref, wu_ref, nw_ref, o_ref):
    y = jnp.concatenate([ya_ref[...], yb_ref[...]], axis=1)
    h1 = h_ref[...] + jnp.dot(y, wo_ref[...], preferred_element_type=F32)
    gate = _sigmoid(jnp.dot(h1.astype(BF16), wg_ref[...], preferred_element_type=F32))
    up = jnp.dot(p_ref[...].astype(BF16), wu_ref[...], preferred_element_type=F32)
    h2 = h1 + up * gate
    if final:
        ms = jnp.mean(h2 * h2, axis=-1, keepdims=True)
        h2 = h2 * lax.rsqrt(ms + NORM_EPS) * nw_ref[...]
    o_ref[...] = h2


def _outproj(h, ya, yb, p, layer, w_out, w_gate, w_up, final_nw, final):
    tokens, d_model = h.shape
    half = ya.shape[1]
    ple = p.shape[2]
    resident = lambda w: pl.BlockSpec((None,) + w.shape[1:], lambda i: (layer, 0, 0),
                                      pipeline_mode=pl.Buffered(1))
    return pl.pallas_call(
        functools.partial(_outproj_kernel, final),
        grid=(tokens // OUT_ROWS,),
        in_specs=[
            pl.BlockSpec((OUT_ROWS, d_model), lambda i: (i, 0)),
            pl.BlockSpec((OUT_ROWS, half), lambda i: (i, 0)),
            pl.BlockSpec((OUT_ROWS, half), lambda i: (i, 0)),
            pl.BlockSpec((None, OUT_ROWS, ple), lambda i: (layer, i, 0)),
            resident(w_out),
            resident(w_gate),
            resident(w_up),
            pl.BlockSpec(final_nw.shape, lambda i: (0, 0)),
        ],
        out_specs=pl.BlockSpec((OUT_ROWS, d_model), lambda i: (i, 0)),
        out_shape=jax.ShapeDtypeStruct((tokens, d_model), F32),
        compiler_params=pltpu.CompilerParams(
            dimension_semantics=("arbitrary",),
            vmem_limit_bytes=VMEM_LIMIT_BYTES),
        name="outproj",
    )(h, ya, yb, p, w_out, w_gate, w_up, final_nw)


def kernel(x, p, norm_w, w_in, dn_conv_w, dn_A_log, dn_dt_bias, dn_norm_w, hg_lb_logits, hg_norm_w,
           w_out, w_ple_up, w_ple_gate, final_norm_w):
    batch, seq, d_model = x.shape
    depth = w_in.shape[0]
    dn_heads = dn_A_log.shape[1]
    dn_width = dn_heads * HEAD_DIM
    hg_width = hg_lb_logits.shape[1]
    hg_heads = hg_width // HEAD_DIM
    tokens = batch * seq
    gate0 = 4 * dn_width
    gate1 = gate0 + 2 * dn_heads
    hg_col0 = gate0 // HEAD_DIM
    assert dn_heads % DN_GROUP == 0 and seq % DN_ROWS == 0 and seq % HG_ROWS == 0

    sums_np, masks_np = _level_constants()
    col_sel_np, row_cum_np = _pair_constants()
    sums = jnp.asarray(sums_np, dtype=BF16)
    masks = jnp.asarray(masks_np)
    masks_pair = jnp.asarray(np.concatenate([masks_np, masks_np], axis=2))
    col_sel = jnp.asarray(col_sel_np, dtype=BF16)
    row_cum = jnp.asarray(row_cum_np, dtype=BF16)

    w_in_t = jnp.swapaxes(w_in, 1, 2).astype(BF16)
    w_out_b, w_gate_b, w_up_b = w_out.astype(BF16), w_ple_gate.astype(BF16), w_ple_up.astype(BF16)

    h = x.reshape(tokens, d_model)
    for l in range(depth):
        proj, bat = _inproj(h, norm_w[l][None, :], w_in_t, l, gate0, gate1 - gate0)
        bat4 = bat.reshape(2, dn_heads // DN_GROUP, DN_GROUP, tokens)
        ya = _deltanet(proj, bat4, dn_conv_w[l], dn_A_log[l], dn_dt_bias[l], dn_norm_w[l][None, :],
                       masks_pair, col_sel, row_cum, batch, seq, dn_heads)
        yb = _hgrn2(proj, hg_lb_logits, hg_norm_w[l][None, :], sums, masks, l, hg_col0,
                    batch, seq, hg_heads)
        h = _outproj(h, ya, yb, p.reshape(depth, tokens, -1), l, w_out_b, w_gate_b, w_up_b,
                     final_norm_w[None, :], l == depth - 1)
    return h.reshape(batch, seq, d_model)
```

```python
import functools

import jax
import jax.numpy as jnp
import numpy as np
from jax import lax
from jax.experimental import pallas as pl
from jax.experimental.pallas import tpu as pltpu

F32 = jnp.float32
BF16 = jnp.bfloat16

HEAD_DIM = 128
CHUNK = 64
PAIR = 2 * CHUNK
CONV_WIDTH = 4
NORM_EPS = 1e-6
L2_EPS = 1e-6
N_LEVELS = 6
VMEM_LIMIT_BYTES = 56 * 1024 * 1024

IN_ROWS = 1024
IN_COLS = 1024
OUT_ROWS = 256
DN_GROUP = 4
DN_ROWS = 4 * CHUNK


def _bdot(a, b):
    return jnp.dot(a.astype(BF16), b.astype(BF16), preferred_element_type=F32)


def _bdot_nt(a, b):
    return lax.dot_general(a.astype(BF16), b.astype(BF16), (((1,), (1,)), ((), ())),
                           preferred_element_type=F32)


def _bdot_tn(a, b):
    return lax.dot_general(a.astype(BF16), b.astype(BF16), (((0,), (0,)), ((), ())),
                           preferred_element_type=F32)


def _split2(x):
    hi = x.astype(BF16)
    lo = (x - hi.astype(F32)).astype(BF16)
    return hi, lo


def _sigmoid_pair(x):
    e = jnp.exp(-jnp.abs(x))
    r = 1.0 / (1.0 + e)
    er = e * r
    pos = x >= 0
    return jnp.where(pos, r, er), jnp.where(pos, er, r)


def _sigmoid(x):
    return 1.0 / (1.0 + jnp.exp(-x))


def _silu(x):
    return x * _sigmoid(x)


def _softplus(x):
    return jnp.maximum(x, 0.0) + jnp.log1p(jnp.exp(-jnp.abs(x)))


def _gated_head_norm(o, z, w):
    o = o * lax.rsqrt(jnp.mean(o * o, axis=-1, keepdims=True) + NORM_EPS) * w
    return o * _silu(z)


def _inproj_kernel(h_ref, nw_ref, w_ref, wbat_ref, proj_ref, bat_ref, hn_ref):
    contract_last = (((1,), (1,)), ((), ()))

    @pl.when(pl.program_id(1) == 0)
    def _():
        x = h_ref[...]
        ms = jnp.mean(x * x, axis=-1, keepdims=True)
        hn = (x * lax.rsqrt(ms + NORM_EPS) * nw_ref[...]).astype(BF16)
        hn_ref[...] = hn
        bat_ref[...] = lax.dot_general(wbat_ref[0], hn, contract_last, preferred_element_type=F32)

    proj_ref[...] = lax.dot_general(hn_ref[...], w_ref[0], contract_last, preferred_element_type=F32)


def _inproj(h, nw, w_t, layer, gate0, n_gate):
    tokens, d_model = h.shape
    width = w_t.shape[1] - n_gate
    assert gate0 % IN_COLS == 0 and width % IN_COLS == 0
    skip_from = gate0 // IN_COLS
    grid = (tokens // IN_ROWS, width // IN_COLS)
    return pl.pallas_call(
        _inproj_kernel,
        grid=grid,
        in_specs=[
            pl.BlockSpec((IN_ROWS, d_model), lambda i, j: (i, 0)),
            pl.BlockSpec((1, d_model), lambda i, j: (0, 0)),
            pl.BlockSpec((pl.Element(1), pl.Element(IN_COLS), pl.Element(d_model)),
                         lambda i, j: (layer, (j * (IN_COLS // n_gate) + (j >= skip_from).astype(jnp.int32))
                                       * n_gate, 0)),
            pl.BlockSpec((pl.Element(1), pl.Element(n_gate), pl.Element(d_model)),
                         lambda i, j: (layer, gate0, 0)),
        ],
        out_specs=[
            pl.BlockSpec((IN_ROWS, IN_COLS), lambda i, j: (i, j)),
            pl.BlockSpec((n_gate, IN_ROWS), lambda i, j: (0, i)),
        ],
        out_shape=[
            jax.ShapeDtypeStruct((tokens, width), F32),
            jax.ShapeDtypeStruct((n_gate, tokens), F32),
        ],
        scratch_shapes=[pltpu.VMEM((IN_ROWS, d_model), BF16)],
        compiler_params=pltpu.CompilerParams(
            dimension_semantics=("arbitrary", "arbitrary"),
            vmem_limit_bytes=VMEM_LIMIT_BYTES),
        name="inproj",
    )(h, nw, w_t, w_t)


def _level_constants():
    t = np.arange(CHUNK)
    sums = [(t[None, :] <= t[:, None])]
    masks = [np.eye(CHUNK, dtype=bool)]
    for lvl in range(1, N_LEVELS + 1):
        size, half = 1 << lvl, 1 << (lvl - 1)
        ref = (t // size) * size + half - 1
        sums.append(t[None, :] <= ref[:, None])
        same = (t[:, None] // size) == (t[None, :] // size)
        upper_r = ((t // half) % 2 == 1)[:, None]
        lower_s = ((t // half) % 2 == 0)[None, :]
        masks.append(same & upper_r & lower_s)
    sums.append(np.ones((CHUNK, CHUNK), dtype=bool))
    sums = np.concatenate(sums, axis=0).astype(np.float32)
    sums = np.concatenate([sums, sums], axis=1)
    masks = np.stack(masks, axis=0).astype(np.float32)
    assert masks.sum(axis=0).tolist() == np.tril(np.ones((CHUNK, CHUNK))).tolist()
    return sums, masks


def _pair_constants():
    k = np.arange(2 * PAIR)
    n2 = np.arange(2 * PAIR)
    n1 = np.arange(PAIR)
    k_chunk = (k % PAIR) // CHUNK
    k_pos = k % CHUNK
    col_sel = (k_chunk[:, None] == (n2 // PAIR)[None, :])
    row_cum = (k_chunk[:, None] == (n1 // CHUNK)[None, :]) & (k_pos[:, None] <= (n1 % CHUNK)[None, :])
    return col_sel.astype(np.float32), row_cum.astype(np.float32)


def _conv_silu(x_ref, w_ref, buf_ref, lanes):
    assert CONV_WIDTH == 4
    rows = x_ref.shape[0]
    x = x_ref[:, lanes]
    xe = jnp.concatenate([buf_ref[pl.ds(0, 8), lanes], x], axis=0)
    buf_ref[pl.ds(0, 8), lanes] = x[rows - 8:rows, :]
    w = w_ref[:, lanes]
    x1 = pltpu.roll(xe, 1, axis=0)
    near = w[3:4, :] * xe + w[2:3, :] * x1
    far = w[1:2, :] * xe + w[0:1, :] * x1
    acc = near + pltpu.roll(far, 2, axis=0)
    return _silu(acc[8:, :])


def _mixer_kernel(layer, alog_ref, dtb_ref, q_ref, k_ref, v_ref, z_ref, b_ref, a_ref,
                  wq_ref, wk_ref, wv_ref, nw_ref, masks_ref, colsel_ref, rowcum_ref,
                  hq_ref, hf_ref, hi_ref, hz_ref, lbl_ref, hnw_ref, sums_ref, hmasks_ref,
                  y_ref, yb_ref,
                  state_ref, qbuf_ref, kbuf_ref, vbuf_ref, hstate_ref):
    group = pl.program_id(1)
    width = q_ref.shape[1]

    @pl.when(pl.program_id(2) == 0)
    def _():
        hstate_ref[...] = jnp.zeros_like(hstate_ref)
        state_ref[...] = jnp.zeros_like(state_ref)
        qbuf_ref[pl.ds(0, 8), :] = jnp.zeros((8, width), F32)
        kbuf_ref[pl.ds(0, 8), :] = jnp.zeros((8, width), F32)
        vbuf_ref[pl.ds(0, 8), :] = jnp.zeros((8, width), F32)

    rows = q_ref.shape[0]
    n_pairs = rows // PAIR
    heads = range(DN_GROUP)
    units = [(j, p) for j in heads for p in range(n_pairs)]
    n_units = len(units)

    lane = lax.broadcasted_iota(jnp.int32, (CHUNK, PAIR), 1)
    row = lax.broadcasted_iota(jnp.int32, (CHUNK, PAIR), 0)
    col = lane & (CHUNK - 1)
    left = lane < CHUNK
    causal = col <= row
    strict = col < row
    eye = col == row
    left_b = jnp.where(left, 1.0, 0.0).astype(BF16)
    right_b = jnp.where(left, 0.0, 1.0).astype(BF16)
    level_b = [masks_ref[lvl].astype(BF16) for lvl in range(N_LEVELS + 1)]
    col_sel = colsel_ref[...]
    row_cum = rowcum_ref[...]

    def block_diag(m):
        return jnp.concatenate([m * left_b, m * right_b], axis=0)

    def pair_dot3(x_hi, x_lo, y_hi, y_lo):
        bd_hi = block_diag(y_hi)
        both = jnp.dot(x_hi, jnp.concatenate([bd_hi, block_diag(y_lo)], axis=1),
                       preferred_element_type=F32)
        return (both[:, 0:PAIR] + both[:, PAIR:2 * PAIR]
                + jnp.dot(x_lo, bd_hi, preferred_element_type=F32))

    q_h, k_h, v_h, beta_rows, g_rows = [], [], [], [], []
    for j in heads:
        sl = slice(j * HEAD_DIM, (j + 1) * HEAD_DIM)
        qj = _conv_silu(q_ref, wq_ref, qbuf_ref, sl)
        kj = _conv_silu(k_ref, wk_ref, kbuf_ref, sl)
        v_h.append(_conv_silu(v_ref, wv_ref, vbuf_ref, sl))
        q_h.append(qj * (lax.rsqrt(jnp.sum(qj * qj, axis=-1, keepdims=True) + L2_EPS) * (HEAD_DIM ** -0.5)))
        k_h.append(kj * lax.rsqrt(jnp.sum(kj * kj, axis=-1, keepdims=True) + L2_EPS))
        head = group * DN_GROUP + j
        beta_rows.append(_sigmoid(b_ref[j:j + 1, :]))
        rate = jnp.exp(jnp.full((1, rows), alog_ref[head], F32))
        g_rows.append(-rate * _softplus(a_ref[j:j + 1, :] + dtb_ref[head]))

    cols, g_rowcums = [], []
    for j, p in units:
        ps = slice(p * PAIR, (p + 1) * PAIR)
        g_b = jnp.broadcast_to(g_rows[j][:, ps], (CHUNK, PAIR))
        beta_b = jnp.broadcast_to(beta_rows[j][:, ps], (CHUNK, PAIR))
        lhs = jnp.concatenate([jnp.where(causal, g_b, 0.0), jnp.where(eye, beta_b, 0.0), g_b], axis=0)
        hi, lo = _split2(lhs)
        hilo = jnp.concatenate([hi, lo], axis=1)
        cols.append(jnp.dot(hilo, col_sel, preferred_element_type=F32))
        g_rowcums.append(jnp.dot(hilo[2 * CHUNK:3 * CHUNK], row_cum,
                                 preferred_element_type=F32))

    def stacked(u, block):
        c = cols[u][block * CHUNK:(block + 1) * CHUNK]
        return jnp.concatenate([c[:, 0:HEAD_DIM], c[:, HEAD_DIM:2 * HEAD_DIM]], axis=0)

    g_cols = [stacked(u, 0) for u in range(n_units)]
    beta_cols = [stacked(u, 1) for u in range(n_units)]
    g_lasts = [stacked(u, 2) for u in range(n_units)]
    exp_gs = [jnp.exp(g) for g in g_cols]
    q_u = [q_h[j][p * PAIR:(p + 1) * PAIR] for j, p in units]
    k_u = [k_h[j][p * PAIR:(p + 1) * PAIR] for j, p in units]
    v_u = [v_h[j][p * PAIR:(p + 1) * PAIR] for j, p in units]
    k_betas = [k_u[u] * beta_cols[u] for u in range(n_units)]

    prods = [_bdot_nt(jnp.concatenate([k_betas[u], q_u[u]], axis=0), k_u[u]) for u in range(n_units)]

    a_his, a_los, qks, invs = [], [], [], []
    for u in range(n_units):
        g_pair = jnp.where(left, g_cols[u][0:CHUNK], g_cols[u][CHUNK:PAIR])
        diff = g_pair - g_rowcums[u]
        decay = jnp.where(causal, jnp.exp(jnp.where(causal, diff, 0.0)), 0.0)
        pr = prods[u]
        kk = jnp.where(left, pr[0:CHUNK], pr[CHUNK:PAIR])
        qk = jnp.where(left, pr[PAIR:PAIR + CHUNK], pr[PAIR + CHUNK:2 * PAIR])
        a_mat = jnp.where(strict, kk * decay, 0.0)
        a_hi, a_lo = _split2(a_mat)
        a_his.append(a_hi)
        a_los.append(a_lo)
        qks.append(qk * decay)
        invs.append(jnp.where(eye, 1.0, 0.0) - masks_ref[1] * a_mat)

    hgrn2 = _hgrn2_stream(layer, hq_ref, hf_ref, hi_ref, hz_ref, lbl_ref, hnw_ref, sums_ref, hmasks_ref,
                          yb_ref, hstate_ref)

    def advance(n):
        for _ in range(n):
            next(hgrn2, None)

    for lvl in range(2, N_LEVELS + 1):
        t_split = [_split2(t) for t in invs]
        xs = [pair_dot3(a_his[u] * level_b[lvl], a_los[u] * level_b[lvl], *t_split[u])
              for u in range(n_units)]
        advance(2)
        invs = [invs[u] - pair_dot3(*t_split[u], *_split2(xs[u])) for u in range(n_units)]
        advance(1)

    sols, qk_sols, kt_sols = [], [], []
    for u in range(n_units):
        rhs = jnp.concatenate([v_u[u] * beta_cols[u], k_betas[u] * exp_gs[u]], axis=1)
        sols.append(jnp.dot(block_diag(invs[u].astype(BF16)), rhs.astype(BF16),
                            preferred_element_type=F32))
    for _ in hgrn2:
        pass
    for u in range(n_units):
        qk_sols.append(jnp.dot(block_diag(qks[u].astype(BF16)), sols[u].astype(BF16),
                               preferred_element_type=F32))
    for u in range(n_units):
        k_tail = k_u[u] * jnp.exp(g_lasts[u] - g_cols[u])
        kt_sols.append([_bdot_tn(k_tail[c * CHUNK:(c + 1) * CHUNK], sols[u][c * CHUNK:(c + 1) * CHUNK])
                        for c in range(2)])

    states = [state_ref[j] for j in heads]
    outs = [[] for _ in heads]
    for p in range(n_pairs):
        for c in range(2):
            cs = slice(c * CHUNK, (c + 1) * CHUNK)
            for j in heads:
                u = j * n_pairs + p
                kt_sol = kt_sols[u][c]
                lhs = jnp.concatenate(
                    [-kt_sol[:, HEAD_DIM:2 * HEAD_DIM],
                     q_u[u][cs] * exp_gs[u][cs] - qk_sols[u][cs, HEAD_DIM:2 * HEAD_DIM]], axis=0)
                both = _bdot(lhs, states[j])
                tail = jnp.exp(g_lasts[u][c * CHUNK:c * CHUNK + 1, :])
                states[j] = tail * states[j] + both[0:HEAD_DIM] + kt_sol[:, 0:HEAD_DIM]
                outs[j].append(both[HEAD_DIM:HEAD_DIM + CHUNK] + qk_sols[u][cs, 0:HEAD_DIM])

    for j in heads:
        state_ref[j] = states[j]
        sl = slice(j * HEAD_DIM, (j + 1) * HEAD_DIM)
        o = jnp.concatenate(outs[j], axis=0)
        y_ref[:, sl] = _gated_head_norm(o, z_ref[:, sl], nw_ref[...]).astype(y_ref.dtype)


def _hgrn2_stream(layer, q_ref, f_ref, i_ref, z_ref, lbl_ref, nw_ref, sums_ref, masks_ref, y_ref, state_ref):
    rows = q_ref.shape[0]
    n_chunks = rows // CHUNK
    heads = range(q_ref.shape[1] // HEAD_DIM)
    units = [(hd, c) for hd in heads for c in range(n_chunks)]

    logits = lbl_ref[...]
    ex = jnp.exp(logits - jnp.max(logits, axis=0, keepdims=True))
    probs = ex / jnp.sum(ex, axis=0, keepdims=True)
    lb_all = jnp.sum(probs[0:layer + 1], axis=0, keepdims=True) - probs[0:1]
    sums2 = sums_ref[...]

    per_head = {}

    def head_arrays(hd):
        if hd not in per_head:
            sl = slice(hd * HEAD_DIM, (hd + 1) * HEAD_DIM)
            lb = lb_all[:, sl]
            sig, nsig = _sigmoid_pair(f_ref[:, sl])
            per_head[hd] = dict(log_f=jnp.log(lb + (1.0 - lb) * sig), k=(1.0 - lb) * nsig,
                                q=_silu(q_ref[:, sl]), v=i_ref[:, sl])
        return per_head[hd]

    def cumulative(idx):
        hd, c = units[idx]
        hi, lo = _split2(head_arrays(hd)["log_f"][c * CHUNK:(c + 1) * CHUNK])
        return jnp.dot(sums2, jnp.concatenate([hi, lo], axis=0), preferred_element_type=F32)

    def intra_matrix(q_c, k_c, g_all):
        g = g_all[0:CHUNK]
        q_b, k_b = q_c.astype(BF16), k_c.astype(BF16)
        a_mat = masks_ref[0] * _bdot_nt(q_b, k_b)
        for lvl in range(1, N_LEVELS + 1):
            wgt = jnp.exp(-jnp.abs(g - g_all[lvl * CHUNK:(lvl + 1) * CHUNK])).astype(BF16)
            a_mat = a_mat + masks_ref[lvl] * _bdot_nt(q_b * wgt, k_b * wgt)
        return a_mat

    states = {hd: state_ref[hd] for hd in heads}
    outs = {hd: [] for hd in heads}

    def finish(hd, c, a_mat, inc, g, g_last):
        arrs = head_arrays(hd)
        cs = slice(c * CHUNK, (c + 1) * CHUNK)
        outs[hd].append(_bdot_nt(arrs["q"][cs] * jnp.exp(g), states[hd]) + _bdot(a_mat, arrs["v"][cs]))
        states[hd] = jnp.exp(g_last[0:1, :]) * states[hd] + inc
        if c == n_chunks - 1:
            sl = slice(hd * HEAD_DIM, (hd + 1) * HEAD_DIM)
            state_ref[hd] = states[hd]
            o = jnp.concatenate(outs[hd], axis=0)
            y_ref[:, sl] = _gated_head_norm(o, z_ref[:, sl], nw_ref[...]).astype(y_ref.dtype)

    LOOKAHEAD = 2
    g_alls = {idx: cumulative(idx) for idx in range(min(LOOKAHEAD, len(units)))}
    pending = None
    for idx, (hd, c) in enumerate(units):
        arrs = head_arrays(hd)
        cs = slice(c * CHUNK, (c + 1) * CHUNK)
        g_all = g_alls.pop(idx)
        g = g_all[0:CHUNK]
        g_last = g_all[(N_LEVELS + 1) * CHUNK:(N_LEVELS + 2) * CHUNK]
        a_mat = intra_matrix(arrs["q"][cs], arrs["k"][cs], g_all)
        if idx + LOOKAHEAD < len(units):
            g_alls[idx + LOOKAHEAD] = cumulative(idx + LOOKAHEAD)
        inc = _bdot_tn(arrs["v"][cs], arrs["k"][cs] * jnp.exp(g_last - g))
        if pending is not None:
            finish(*pending)
        pending = (hd, c, a_mat, inc, g, g_last)
        yield
    finish(*pending)
    yield


def _mixer(proj, bat4, conv_w, a_log, dt_bias, dn_norm_w, masks_pair, col_sel, row_cum,
           lb_logits, hg_norm_w, sums, masks, layer, batch, seq, n_heads):
    tokens = proj.shape[0]
    depth = lb_logits.shape[0]
    steps = seq // DN_ROWS
    groups = n_heads // DN_GROUP
    width = DN_GROUP * HEAD_DIM
    row_map = lambda col0: (lambda b, g, t: (b * steps + t, col0 + g))
    gate_map = lambda kind: (lambda b, g, t: (kind, g, 0, b * steps + t))
    conv_map = lambda col0: (lambda b, g, t: (0, col0 + g))
    const2 = lambda b, g, t: (0, 0)
    const3 = lambda b, g, t: (0, 0, 0)
    smem = pl.BlockSpec(memory_space=pltpu.SMEM)
    tile = lambda k: pl.BlockSpec((DN_ROWS, width), row_map(k * groups))
    y_shape = jax.ShapeDtypeStruct((tokens, n_heads * HEAD_DIM), BF16)
    return pl.pallas_call(
        functools.partial(_mixer_kernel, layer),
        grid=(batch, groups, steps),
        in_specs=[
            smem, smem,
            tile(0), tile(1), tile(2), tile(3),
            pl.BlockSpec((None, None, DN_GROUP, DN_ROWS), gate_map(0)),
            pl.BlockSpec((None, None, DN_GROUP, DN_ROWS), gate_map(1)),
            pl.BlockSpec((CONV_WIDTH, width), conv_map(0)),
            pl.BlockSpec((CONV_WIDTH, width), conv_map(groups)),
            pl.BlockSpec((CONV_WIDTH, width), conv_map(2 * groups)),
            pl.BlockSpec((1, HEAD_DIM), const2),
            pl.BlockSpec(masks_pair.shape, const3),
            pl.BlockSpec(col_sel.shape, const2),
            pl.BlockSpec(row_cum.shape, const2),
            tile(4), tile(5), tile(6), tile(7),
            pl.BlockSpec((depth, width), lambda b, g, t: (0, g)),
            pl.BlockSpec((1, HEAD_DIM), const2),
            pl.BlockSpec(sums.shape, const2),
            pl.BlockSpec(masks.shape, const3),
        ],
        out_specs=[pl.BlockSpec((DN_ROWS, width), row_map(0)), pl.BlockSpec((DN_ROWS, width), row_map(0))],
        out_shape=[y_shape, y_shape],
        scratch_shapes=[
            pltpu.VMEM((DN_GROUP, HEAD_DIM, HEAD_DIM), F32),
            pltpu.VMEM((8, width), F32),
            pltpu.VMEM((8, width), F32),
            pltpu.VMEM((8, width), F32),
            pltpu.VMEM((DN_GROUP, HEAD_DIM, HEAD_DIM), F32),
        ],
        compiler_params=pltpu.CompilerParams(
            dimension_semantics=("arbitrary", "arbitrary", "arbitrary"),
            vmem_limit_bytes=VMEM_LIMIT_BYTES),
        name="mixer",
    )(a_log, dt_bias, proj, proj, proj, proj, bat4, bat4, conv_w, conv_w, conv_w, dn_norm_w,
      masks_pair, col_sel, row_cum, proj, proj, proj, proj, lb_logits, hg_norm_w, sums, masks)


def _outproj_kernel(final, h_ref, ya_ref, yb_ref, p_ref, wo_ref, wg_ref, wu_ref, nw_ref, o_ref):
    y = jnp.concatenate([ya_ref[...], yb_ref[...]], axis=1)
    h1 = h_ref[...] + jnp.dot(y, wo_ref[...], preferred_element_type=F32)
    gate = _sigmoid(jnp.dot(h1.astype(BF16), wg_ref[...], preferred_element_type=F32))
    up = jnp.dot(p_ref[...].astype(BF16), wu_ref[...], preferred_element_type=F32)
    h2 = h1 + up * gate
    if final:
        ms = jnp.mean(h2 * h2, axis=-1, keepdims=True)
        h2 = h2 * lax.rsqrt(ms + NORM_EPS) * nw_ref[...]
    o_ref[...] = h2


def _outproj(h, ya, yb, p, layer, w_out, w_gate, w_up, final_nw, final):
    tokens, d_model = h.shape
    half = ya.shape[1]
    ple = p.shape[2]
    resident = lambda w: pl.BlockSpec((None,) + w.shape[1:], lambda i: (layer, 0, 0),
                                      pipeline_mode=pl.Buffered(1))
    return pl.pallas_call(
        functools.partial(_outproj_kernel, final),
        grid=(tokens // OUT_ROWS,),
        in_specs=[
            pl.BlockSpec((OUT_ROWS, d_model), lambda i: (i, 0)),
            pl.BlockSpec((OUT_ROWS, half), lambda i: (i, 0)),
            pl.BlockSpec((OUT_ROWS, half), lambda i: (i, 0)),
            pl.BlockSpec((None, OUT_ROWS, ple), lambda i: (layer, i, 0)),
            resident(w_out),
            resident(w_gate),
            resident(w_up),
            pl.BlockSpec(final_nw.shape, lambda i: (0, 0)),
        ],
        out_specs=pl.BlockSpec((OUT_ROWS, d_model), lambda i: (i, 0)),
        out_shape=jax.ShapeDtypeStruct((tokens, d_model), F32),
        compiler_params=pltpu.CompilerParams(
            dimension_semantics=("arbitrary",),
            vmem_limit_bytes=VMEM_LIMIT_BYTES),
        name="outproj",
    )(h, ya, yb, p, w_out, w_gate, w_up, final_nw)


def kernel(x, p, norm_w, w_in, dn_conv_w, dn_A_log, dn_dt_bias, dn_norm_w, hg_lb_logits, hg_norm_w,
           w_out, w_ple_up, w_ple_gate, final_norm_w):
    batch, seq, d_model = x.shape
    depth = w_in.shape[0]
    dn_heads = dn_A_log.shape[1]
    dn_width = dn_heads * HEAD_DIM
    tokens = batch * seq
    gate0 = 4 * dn_width
    gate1 = gate0 + 2 * dn_heads
    assert hg_lb_logits.shape[1] == dn_width, "the mixer steps both head groups in lockstep"
    assert dn_heads % DN_GROUP == 0 and seq % DN_ROWS == 0

    sums_np, masks_np = _level_constants()
    col_sel_np, row_cum_np = _pair_constants()
    sums = jnp.asarray(sums_np, dtype=BF16)
    masks = jnp.asarray(masks_np)
    masks_pair = jnp.asarray(np.concatenate([masks_np, masks_np], axis=2))
    col_sel = jnp.asarray(col_sel_np, dtype=BF16)
    row_cum = jnp.asarray(row_cum_np, dtype=BF16)

    w_in_t = jnp.swapaxes(w_in, 1, 2).astype(BF16)
    w_out_b, w_gate_b, w_up_b = w_out.astype(BF16), w_ple_gate.astype(BF16), w_ple_up.astype(BF16)

    h = x.reshape(tokens, d_model)
    for l in range(depth):
        proj, bat = _inproj(h, norm_w[l][None, :], w_in_t, l, gate0, gate1 - gate0)
        bat4 = bat.reshape(2, dn_heads // DN_GROUP, DN_GROUP, tokens)
        ya, yb = _mixer(proj, bat4, dn_conv_w[l], dn_A_log[l], dn_dt_bias[l], dn_norm_w[l][None, :],
                        masks_pair, col_sel, row_cum, hg_lb_logits, hg_norm_w[l][None, :], sums, masks, l,
                        batch, seq, dn_heads)
        h = _outproj(h, ya, yb, p.reshape(depth, tokens, -1), l, w_out_b, w_gate_b, w_up_b,
                     final_norm_w[None, :], l == depth - 1)
    return h.reshape(batch, seq, d_model)
```

```python
import functools

import jax
import jax.numpy as jnp
import numpy as np
from jax import lax
from jax.experimental import pallas as pl
from jax.experimental.pallas import tpu as pltpu

F32 = jnp.float32
BF16 = jnp.bfloat16

HEAD_DIM = 128
CHUNK = 64
PAIR = 2 * CHUNK
CONV_WIDTH = 4
NORM_EPS = 1e-6
L2_EPS = 1e-6
N_LEVELS = 6
VMEM_LIMIT_BYTES = 56 * 1024 * 1024

IN_ROWS = 1024
IN_COLS = 1024
OUT_ROWS = 256
DN_GROUP = 4
DN_ROWS = 4 * CHUNK


def _bdot(a, b):
    return jnp.dot(a.astype(BF16), b.astype(BF16), preferred_element_type=F32)


def _bdot_nt(a, b):
    return lax.dot_general(a.astype(BF16), b.astype(BF16), (((1,), (1,)), ((), ())),
                           preferred_element_type=F32)


def _bdot_tn(a, b):
    return lax.dot_general(a.astype(BF16), b.astype(BF16), (((0,), (0,)), ((), ())),
                           preferred_element_type=F32)


def _split2(x):
    hi = x.astype(BF16)
    lo = (x - hi.astype(F32)).astype(BF16)
    return hi, lo


def _sigmoid_pair(x):
    e = jnp.exp(-jnp.abs(x))
    r = 1.0 / (1.0 + e)
    er = e * r
    pos = x >= 0
    return jnp.where(pos, r, er), jnp.where(pos, er, r)


def _sigmoid(x):
    return 1.0 / (1.0 + jnp.exp(-x))


def _silu(x):
    return x * _sigmoid(x)


def _softplus(x):
    return jnp.maximum(x, 0.0) + jnp.log1p(jnp.exp(-jnp.abs(x)))


def _gated_head_norm(o, z, w):
    o = o * lax.rsqrt(jnp.mean(o * o, axis=-1, keepdims=True) + NORM_EPS) * w
    return o * _silu(z)


def _inproj_kernel(h_ref, nw_ref, w_ref, wbat_ref, proj_ref, bat_ref, hn_ref):
    contract_last = (((1,), (1,)), ((), ()))

    @pl.when(pl.program_id(1) == 0)
    def _():
        x = h_ref[...]
        ms = jnp.mean(x * x, axis=-1, keepdims=True)
        hn = (x * lax.rsqrt(ms + NORM_EPS) * nw_ref[...]).astype(BF16)
        hn_ref[...] = hn
        bat_ref[...] = lax.dot_general(wbat_ref[0], hn, contract_last, preferred_element_type=F32)

    proj_ref[...] = lax.dot_general(hn_ref[...], w_ref[0], contract_last, preferred_element_type=F32)


def _inproj(h, nw, w_t, layer, gate0, n_gate):
    tokens, d_model = h.shape
    width = w_t.shape[1] - n_gate
    assert gate0 % IN_COLS == 0 and width % IN_COLS == 0
    skip_from = gate0 // IN_COLS
    grid = (tokens // IN_ROWS, width // IN_COLS)
    return pl.pallas_call(
        _inproj_kernel,
        grid=grid,
        in_specs=[
            pl.BlockSpec((IN_ROWS, d_model), lambda i, j: (i, 0)),
            pl.BlockSpec((1, d_model), lambda i, j: (0, 0)),
            pl.BlockSpec((pl.Element(1), pl.Element(IN_COLS), pl.Element(d_model)),
                         lambda i, j: (layer, (j * (IN_COLS // n_gate) + (j >= skip_from).astype(jnp.int32))
                                       * n_gate, 0)),
            pl.BlockSpec((pl.Element(1), pl.Element(n_gate), pl.Element(d_model)),
                         lambda i, j: (layer, gate0, 0)),
        ],
        out_specs=[
            pl.BlockSpec((IN_ROWS, IN_COLS), lambda i, j: (i, j)),
            pl.BlockSpec((n_gate, IN_ROWS), lambda i, j: (0, i)),
        ],
        out_shape=[
            jax.ShapeDtypeStruct((tokens, width), F32),
            jax.ShapeDtypeStruct((n_gate, tokens), F32),
        ],
        scratch_shapes=[pltpu.VMEM((IN_ROWS, d_model), BF16)],
        compiler_params=pltpu.CompilerParams(
            dimension_semantics=("arbitrary", "arbitrary"),
            vmem_limit_bytes=VMEM_LIMIT_BYTES),
        name="inproj",
    )(h, nw, w_t, w_t)


def _level_constants():
    t = np.arange(CHUNK)
    sums = [(t[None, :] <= t[:, None])]
    masks = [np.eye(CHUNK, dtype=bool)]
    for lvl in range(1, N_LEVELS + 1):
        size, half = 1 << lvl, 1 << (lvl - 1)
        ref = (t // size) * size + half - 1
        sums.append(t[None, :] <= ref[:, None])
        same = (t[:, None] // size) == (t[None, :] // size)
        upper_r = ((t // half) % 2 == 1)[:, None]
        lower_s = ((t // half) % 2 == 0)[None, :]
        masks.append(same & upper_r & lower_s)
    sums.append(np.ones((CHUNK, CHUNK), dtype=bool))
    sums = np.concatenate(sums, axis=0).astype(np.float32)
    sums = np.concatenate([sums, sums], axis=1)
    masks = np.stack(masks, axis=0).astype(np.float32)
    assert masks.sum(axis=0).tolist() == np.tril(np.ones((CHUNK, CHUNK))).tolist()
    return sums, masks


def _pair_constants():
    k = np.arange(2 * PAIR)
    n2 = np.arange(2 * PAIR)
    n1 = np.arange(PAIR)
    k_chunk = (k % PAIR) // CHUNK
    k_pos = k % CHUNK
    col_sel = (k_chunk[:, None] == (n2 // PAIR)[None, :])
    row_cum = (k_chunk[:, None] == (n1 // CHUNK)[None, :]) & (k_pos[:, None] <= (n1 % CHUNK)[None, :])
    return col_sel.astype(np.float32), row_cum.astype(np.float32)


def _conv_silu(x_ref, w_ref, buf_ref, lanes):
    assert CONV_WIDTH == 4
    rows = x_ref.shape[0]
    x = x_ref[:, lanes]
    xe = jnp.concatenate([buf_ref[pl.ds(0, 8), lanes], x], axis=0)
    buf_ref[pl.ds(0, 8), lanes] = x[rows - 8:rows, :]
    w = w_ref[:, lanes]
    x1 = pltpu.roll(xe, 1, axis=0)
    near = w[3:4, :] * xe + w[2:3, :] * x1
    far = w[1:2, :] * xe + w[0:1, :] * x1
    acc = near + pltpu.roll(far, 2, axis=0)
    return _silu(acc[8:, :])


def _mixer_kernel(layer, alog_ref, dtb_ref, q_ref, k_ref, v_ref, z_ref, b_ref, a_ref,
                  wq_ref, wk_ref, wv_ref, nw_ref, masks_ref, colsel_ref, rowcum_ref,
                  hq_ref, hf_ref, hi_ref, hz_ref, lbl_ref, hnw_ref, sums_ref, hmasks_ref,
                  y_ref, yb_ref,
                  state_ref, qbuf_ref, kbuf_ref, vbuf_ref, hstate_ref):
    group = pl.program_id(1)
    width = q_ref.shape[1]

    @pl.when(pl.program_id(2) == 0)
    def _():
        hstate_ref[...] = jnp.zeros_like(hstate_ref)
        state_ref[...] = jnp.zeros_like(state_ref)
        qbuf_ref[pl.ds(0, 8), :] = jnp.zeros((8, width), F32)
        kbuf_ref[pl.ds(0, 8), :] = jnp.zeros((8, width), F32)
        vbuf_ref[pl.ds(0, 8), :] = jnp.zeros((8, width), F32)

    rows = q_ref.shape[0]
    n_pairs = rows // PAIR
    heads = range(DN_GROUP)
    units = [(j, p) for j in heads for p in range(n_pairs)]
    n_units = len(units)

    lane = lax.broadcasted_iota(jnp.int32, (CHUNK, PAIR), 1)
    row = lax.broadcasted_iota(jnp.int32, (CHUNK, PAIR), 0)
    col = lane & (CHUNK - 1)
    left = lane < CHUNK
    causal = col <= row
    strict = col < row
    eye = col == row
    left_b = jnp.where(left, 1.0, 0.0).astype(BF16)
    right_b = jnp.where(left, 0.0, 1.0).astype(BF16)
    level_b = [masks_ref[lvl].astype(BF16) for lvl in range(N_LEVELS + 1)]
    col_sel = colsel_ref[...]
    row_cum = rowcum_ref[...]

    def block_diag(m):
        return jnp.concatenate([m * left_b, m * right_b], axis=0)

    def pair_dot3(x_hi, x_lo, y_hi, y_lo):
        bd_hi = block_diag(y_hi)
        both = jnp.dot(x_hi, jnp.concatenate([bd_hi, block_diag(y_lo)], axis=1),
                       preferred_element_type=F32)
        return (both[:, 0:PAIR] + both[:, PAIR:2 * PAIR]
                + jnp.dot(x_lo, bd_hi, preferred_element_type=F32))

    q_h, k_h, v_h, beta_rows, g_rows = [], [], [], [], []
    for j in heads:
        sl = slice(j * HEAD_DIM, (j + 1) * HEAD_DIM)
        qj = _conv_silu(q_ref, wq_ref, qbuf_ref, sl)
        kj = _conv_silu(k_ref, wk_ref, kbuf_ref, sl)
        v_h.append(_conv_silu(v_ref, wv_ref, vbuf_ref, sl))
        q_h.append(qj * (lax.rsqrt(jnp.sum(qj * qj, axis=-1, keepdims=True) + L2_EPS) * (HEAD_DIM ** -0.5)))
        k_h.append(kj * lax.rsqrt(jnp.sum(kj * kj, axis=-1, keepdims=True) + L2_EPS))
        head = group * DN_GROUP + j
        beta_rows.append(_sigmoid(b_ref[j:j + 1, :]))
        rate = jnp.exp(jnp.full((1, rows), alog_ref[head], F32))
        g_rows.append(-rate * _softplus(a_ref[j:j + 1, :] + dtb_ref[head]))

    cols, g_rowcums = [], []
    for j, p in units:
        ps = slice(p * PAIR, (p + 1) * PAIR)
        g_b = jnp.broadcast_to(g_rows[j][:, ps], (CHUNK, PAIR))
        beta_b = jnp.broadcast_to(beta_rows[j][:, ps], (CHUNK, PAIR))
        lhs = jnp.concatenate([jnp.where(causal, g_b, 0.0), jnp.where(eye, beta_b, 0.0), g_b], axis=0)
        hi, lo = _split2(lhs)
        hilo = jnp.concatenate([hi, lo], axis=1)
        cols.append(jnp.dot(hilo, col_sel, preferred_element_type=F32))
        g_rowcums.append(jnp.dot(hilo[2 * CHUNK:3 * CHUNK], row_cum,
                                 preferred_element_type=F32))

    def stacked(u, block):
        c = cols[u][block * CHUNK:(block + 1) * CHUNK]
        return jnp.concatenate([c[:, 0:HEAD_DIM], c[:, HEAD_DIM:2 * HEAD_DIM]], axis=0)

    g_cols = [stacked(u, 0) for u in range(n_units)]
    beta_cols = [stacked(u, 1) for u in range(n_units)]
    g_lasts = [stacked(u, 2) for u in range(n_units)]
    exp_gs = [jnp.exp(g) for g in g_cols]
    q_u = [q_h[j][p * PAIR:(p + 1) * PAIR] for j, p in units]
    k_u = [k_h[j][p * PAIR:(p + 1) * PAIR] for j, p in units]
    v_u = [v_h[j][p * PAIR:(p + 1) * PAIR] for j, p in units]
    k_betas = [k_u[u] * beta_cols[u] for u in range(n_units)]

    prods = [_bdot_nt(jnp.concatenate([k_betas[u], q_u[u]], axis=0), k_u[u]) for u in range(n_units)]

    a_his, a_los, qks, invs = [], [], [], []
    for u in range(n_units):
        g_pair = jnp.where(left, g_cols[u][0:CHUNK], g_cols[u][CHUNK:PAIR])
        diff = g_pair - g_rowcums[u]
        decay = jnp.where(causal, jnp.exp(jnp.where(causal, diff, 0.0)), 0.0)
        pr = prods[u]
        kk = jnp.where(left, pr[0:CHUNK], pr[CHUNK:PAIR])
        qk = jnp.where(left, pr[PAIR:PAIR + CHUNK], pr[PAIR + CHUNK:2 * PAIR])
        a_mat = jnp.where(strict, kk * decay, 0.0)
        a_hi, a_lo = _split2(a_mat)
        a_his.append(a_hi)
        a_los.append(a_lo)
        qks.append(qk * decay)
        invs.append(jnp.where(eye, 1.0, 0.0) - masks_ref[1] * a_mat)

    hgrn2 = _hgrn2_stream(layer, hq_ref, hf_ref, hi_ref, hz_ref, lbl_ref, hnw_ref, sums_ref, hmasks_ref,
                          yb_ref, hstate_ref)

    def advance(n):
        for _ in range(n):
            next(hgrn2, None)

    for lvl in range(2, N_LEVELS + 1):
        t_split = [_split2(t) for t in invs]
        xs = [pair_dot3(a_his[u] * level_b[lvl], a_los[u] * level_b[lvl], *t_split[u])
              for u in range(n_units)]
        advance(1)
        invs = [invs[u] - pair_dot3(*t_split[u], *_split2(xs[u])) for u in range(n_units)]
        advance(1)

    sols, qk_sols, kt_sols = [], [], []
    for u in range(n_units):
        rhs = jnp.concatenate([v_u[u] * beta_cols[u], k_betas[u] * exp_gs[u]], axis=1)
        sols.append(jnp.dot(block_diag(invs[u].astype(BF16)), rhs.astype(BF16),
                            preferred_element_type=F32))
    advance(1)
    for u in range(n_units):
        qk_sols.append(jnp.dot(block_diag(qks[u].astype(BF16)), sols[u].astype(BF16),
                               preferred_element_type=F32))
    for u in range(n_units):
        k_tail = k_u[u] * jnp.exp(g_lasts[u] - g_cols[u])
        kt_sols.append([_bdot_tn(k_tail[c * CHUNK:(c + 1) * CHUNK], sols[u][c * CHUNK:(c + 1) * CHUNK])
                        for c in range(2)])
    advance(2)

    states = [state_ref[j] for j in heads]
    outs = [[] for _ in heads]
    for p in range(n_pairs):
        for c in range(2):
            cs = slice(c * CHUNK, (c + 1) * CHUNK)
            for j in heads:
                u = j * n_pairs + p
                kt_sol = kt_sols[u][c]
                lhs = jnp.concatenate(
                    [-kt_sol[:, HEAD_DIM:2 * HEAD_DIM],
                     q_u[u][cs] * exp_gs[u][cs] - qk_sols[u][cs, HEAD_DIM:2 * HEAD_DIM]], axis=0)
                both = _bdot(lhs, states[j])
                tail = jnp.exp(g_lasts[u][c * CHUNK:c * CHUNK + 1, :])
                states[j] = tail * states[j] + both[0:HEAD_DIM] + kt_sol[:, 0:HEAD_DIM]
                outs[j].append(both[HEAD_DIM:HEAD_DIM + CHUNK] + qk_sols[u][cs, 0:HEAD_DIM])
            advance(1)
    for _ in hgrn2:
        pass

    for j in heads:
        state_ref[j] = states[j]
        sl = slice(j * HEAD_DIM, (j + 1) * HEAD_DIM)
        o = jnp.concatenate(outs[j], axis=0)
        y_ref[:, sl] = _gated_head_norm(o, z_ref[:, sl], nw_ref[...]).astype(y_ref.dtype)


def _hgrn2_stream(layer, q_ref, f_ref, i_ref, z_ref, lbl_ref, nw_ref, sums_ref, masks_ref, y_ref, state_ref):
    rows = q_ref.shape[0]
    n_chunks = rows // CHUNK
    heads = range(q_ref.shape[1] // HEAD_DIM)
    units = [(hd, c) for hd in heads for c in range(n_chunks)]

    logits = lbl_ref[...]
    ex = jnp.exp(logits - jnp.max(logits, axis=0, keepdims=True))
    probs = ex / jnp.sum(ex, axis=0, keepdims=True)
    lb_all = jnp.sum(probs[0:layer + 1], axis=0, keepdims=True) - probs[0:1]
    sums2 = sums_ref[...]

    per_head = {}

    def head_arrays(hd):
        if hd not in per_head:
            sl = slice(hd * HEAD_DIM, (hd + 1) * HEAD_DIM)
            lb = lb_all[:, sl]
            sig, nsig = _sigmoid_pair(f_ref[:, sl])
            per_head[hd] = dict(log_f=jnp.log(lb + (1.0 - lb) * sig), k=(1.0 - lb) * nsig,
                                q=_silu(q_ref[:, sl]), v=i_ref[:, sl])
        return per_head[hd]

    def cumulative(idx):
        hd, c = units[idx]
        hi, lo = _split2(head_arrays(hd)["log_f"][c * CHUNK:(c + 1) * CHUNK])
        return jnp.dot(sums2, jnp.concatenate([hi, lo], axis=0), preferred_element_type=F32)

    states = {hd: state_ref[hd] for hd in heads}
    outs = {hd: [] for hd in heads}

    def finish(hd, c, a_mat, inc, g, g_last):
        arrs = head_arrays(hd)
        cs = slice(c * CHUNK, (c + 1) * CHUNK)
        outs[hd].append(_bdot_nt(arrs["q"][cs] * jnp.exp(g), states[hd]) + _bdot(a_mat, arrs["v"][cs]))
        states[hd] = jnp.exp(g_last[0:1, :]) * states[hd] + inc
        if c == n_chunks - 1:
            sl = slice(hd * HEAD_DIM, (hd + 1) * HEAD_DIM)
            state_ref[hd] = states[hd]
            o = jnp.concatenate(outs[hd], axis=0)
            y_ref[:, sl] = _gated_head_norm(o, z_ref[:, sl], nw_ref[...]).astype(y_ref.dtype)

    LOOKAHEAD = 2
    g_alls = {idx: cumulative(idx) for idx in range(min(LOOKAHEAD, len(units)))}
    pending = None
    for idx, (hd, c) in enumerate(units):
        arrs = head_arrays(hd)
        cs = slice(c * CHUNK, (c + 1) * CHUNK)
        g_all = g_alls.pop(idx)
        g = g_all[0:CHUNK]
        g_last = g_all[(N_LEVELS + 1) * CHUNK:(N_LEVELS + 2) * CHUNK]
        q_b, k_b = arrs["q"][cs].astype(BF16), arrs["k"][cs].astype(BF16)
        a_mat = masks_ref[0] * _bdot_nt(q_b, k_b)
        for lvl in range(1, N_LEVELS + 1):
            wgt = jnp.exp(-jnp.abs(g - g_all[lvl * CHUNK:(lvl + 1) * CHUNK])).astype(BF16)
            a_mat = a_mat + masks_ref[lvl] * _bdot_nt(q_b * wgt, k_b * wgt)
        if idx + LOOKAHEAD < len(units):
            g_alls[idx + LOOKAHEAD] = cumulative(idx + LOOKAHEAD)
        inc = _bdot_tn(arrs["v"][cs], arrs["k"][cs] * jnp.exp(g_last - g))
        if pending is not None:
            finish(*pending)
        pending = (hd, c, a_mat, inc, g, g_last)
        yield
    finish(*pending)
    yield


def _mixer(proj, bat4, conv_w, a_log, dt_bias, dn_norm_w, masks_pair, col_sel, row_cum,
           lb_logits, hg_norm_w, sums, masks, layer, batch, seq, n_heads):
    tokens = proj.shape[0]
    depth = lb_logits.shape[0]
    steps = seq // DN_ROWS
    groups = n_heads // DN_GROUP
    width = DN_GROUP * HEAD_DIM
    row_map = lambda col0: (lambda b, g, t: (b * steps + t, col0 + g))
    gate_map = lambda kind: (lambda b, g, t: (kind, g, 0, b * steps + t))
    conv_map = lambda col0: (lambda b, g, t: (0, col0 + g))
    const2 = lambda b, g, t: (0, 0)
    const3 = lambda b, g, t: (0, 0, 0)
    smem = pl.BlockSpec(memory_space=pltpu.SMEM)
    tile = lambda k: pl.BlockSpec((DN_ROWS, width), row_map(k * groups))
    y_shape = jax.ShapeDtypeStruct((tokens, n_heads * HEAD_DIM), BF16)
    return pl.pallas_call(
        functools.partial(_mixer_kernel, layer),
        grid=(batch, groups, steps),
        in_specs=[
            smem, smem,
            tile(0), tile(1), tile(2), tile(3),
            pl.BlockSpec((None, None, DN_GROUP, DN_ROWS), gate_map(0)),
            pl.BlockSpec((None, None, DN_GROUP, DN_ROWS), gate_map(1)),
            pl.BlockSpec((CONV_WIDTH, width), conv_map(0)),
            pl.BlockSpec((CONV_WIDTH, width), conv_map(groups)),
            pl.BlockSpec((CONV_WIDTH, width), conv_map(2 * groups)),
            pl.BlockSpec((1, HEAD_DIM), const2),
            pl.BlockSpec(masks_pair.shape, const3),
            pl.BlockSpec(col_sel.shape, const2),
            pl.BlockSpec(row_cum.shape, const2),
            tile(4), tile(5), tile(6), tile(7),
            pl.BlockSpec((depth, width), lambda b, g, t: (0, g)),
            pl.BlockSpec((1, HEAD_DIM), const2),
            pl.BlockSpec(sums.shape, const2),
            pl.BlockSpec(masks.shape, const3),
        ],
        out_specs=[pl.BlockSpec((DN_ROWS, width), row_map(0)), pl.BlockSpec((DN_ROWS, width), row_map(0))],
        out_shape=[y_shape, y_shape],
        scratch_shapes=[
            pltpu.VMEM((DN_GROUP, HEAD_DIM, HEAD_DIM), F32),
            pltpu.VMEM((8, width), F32),
            pltpu.VMEM((8, width), F32),
            pltpu.VMEM((8, width), F32),
            pltpu.VMEM((DN_GROUP, HEAD_DIM, HEAD_DIM), F32),
        ],
        compiler_params=pltpu.CompilerParams(
            dimension_semantics=("arbitrary", "arbitrary", "arbitrary"),
            vmem_limit_bytes=VMEM_LIMIT_BYTES),
        name="mixer",
    )(a_log, dt_bias, proj, proj, proj, proj, bat4, bat4, conv_w, conv_w, conv_w, dn_norm_w,
      masks_pair, col_sel, row_cum, proj, proj, proj, proj, lb_logits, hg_norm_w, sums, masks)


def _outproj_kernel(final, h_ref, ya_ref, yb_ref, p_ref, wo_ref, wg_ref, wu_ref, nw_ref, o_ref):
    y = jnp.concatenate([ya_ref[...], yb_ref[...]], axis=1)
    h1 = h_ref[...] + jnp.dot(y, wo_ref[...], preferred_element_type=F32)
    gate = _sigmoid(jnp.dot(h1.astype(BF16), wg_ref[...], preferred_element_type=F32))
    up = jnp.dot(p_ref[...].astype(BF16), wu_ref[...], preferred_element_type=F32)
    h2 = h1 + up * gate
    if final:
        ms = jnp.mean(h2 * h2, axis=-1, keepdims=True)
        h2 = h2 * lax.rsqrt(ms + NORM_EPS) * nw_ref[...]
    o_ref[...] = h2


def _outproj(h, ya, yb, p, layer, w_out, w_gate, w_up, final_nw, final):
    tokens, d_model = h.shape
    half = ya.shape[1]
    ple = p.shape[2]
    resident = lambda w: pl.BlockSpec((None,) + w.shape[1:], lambda i: (layer, 0, 0),
                                      pipeline_mode=pl.Buffered(1))
    return pl.pallas_call(
        functools.partial(_outproj_kernel, final),
        grid=(tokens // OUT_ROWS,),
        in_specs=[
            pl.BlockSpec((OUT_ROWS, d_model), lambda i: (i, 0)),
            pl.BlockSpec((OUT_ROWS, half), lambda i: (i, 0)),
            pl.BlockSpec((OUT_ROWS, half), lambda i: (i, 0)),
            pl.BlockSpec((None, OUT_ROWS, ple), lambda i: (layer, i, 0)),
            resident(w_out),
            resident(w_gate),
            resident(w_up),
            pl.BlockSpec(final_nw.shape, lambda i: (0, 0)),
        ],
        out_specs=pl.BlockSpec((OUT_ROWS, d_model), lambda i: (i, 0)),
        out_shape=jax.ShapeDtypeStruct((tokens, d_model), F32),
        compiler_params=pltpu.CompilerParams(
            dimension_semantics=("arbitrary",),
            vmem_limit_bytes=VMEM_LIMIT_BYTES),
        name="outproj",
    )(h, ya, yb, p, w_out, w_gate, w_up, final_nw)


def kernel(x, p, norm_w, w_in, dn_conv_w, dn_A_log, dn_dt_bias, dn_norm_w, hg_lb_logits, hg_norm_w,
           w_out, w_ple_up, w_ple_gate, final_norm_w):
    batch, seq, d_model = x.shape
    depth = w_in.shape[0]
    dn_heads = dn_A_log.shape[1]
    dn_width = dn_heads * HEAD_DIM
    tokens = batch * seq
    gate0 = 4 * dn_width
    gate1 = gate0 + 2 * dn_heads
    assert hg_lb_logits.shape[1] == dn_width, "the mixer steps both head groups in lockstep"
    assert dn_heads % DN_GROUP == 0 and seq % DN_ROWS == 0

    sums_np, masks_np = _level_constants()
    col_sel_np, row_cum_np = _pair_constants()
    sums = jnp.asarray(sums_np, dtype=BF16)
    masks = jnp.asarray(masks_np)
    masks_pair = jnp.asarray(np.concatenate([masks_np, masks_np], axis=2))
    col_sel = jnp.asarray(col_sel_np, dtype=BF16)
    row_cum = jnp.asarray(row_cum_np, dtype=BF16)

    w_in_t = jnp.swapaxes(w_in, 1, 2).astype(BF16)
    w_out_b, w_gate_b, w_up_b = w_out.astype(BF16), w_ple_gate.astype(BF16), w_ple_up.astype(BF16)

    h = x.reshape(tokens, d_model)
    for l in range(depth):
        proj, bat = _inproj(h, norm_w[l][None, :], w_in_t, l, gate0, gate1 - gate0)
        bat4 = bat.reshape(2, dn_heads // DN_GROUP, DN_GROUP, tokens)
        ya, yb = _mixer(proj, bat4, dn_conv_w[l], dn_A_log[l], dn_dt_bias[l], dn_norm_w[l][None, :],
                        masks_pair, col_sel, row_cum, hg_lb_logits, hg_norm_w[l][None, :], sums, masks, l,
                        batch, seq, dn_heads)
        h = _outproj(h, ya, yb, p.reshape(depth, tokens, -1), l, w_out_b, w_gate_b, w_up_b,
                     final_norm_w[None, :], l == depth - 1)
    return h.reshape(batch, seq, d_model)
```

```python
import functools

import jax
import jax.numpy as jnp
import numpy as np
from jax import lax
from jax.experimental import pallas as pl
from jax.experimental.pallas import tpu as pltpu

F32 = jnp.float32
BF16 = jnp.bfloat16

HEAD_DIM = 128
CHUNK = 64
PAIR = 2 * CHUNK
CONV_WIDTH = 4
NORM_EPS = 1e-6
L2_EPS = 1e-6
N_LEVELS = 6
VMEM_LIMIT_BYTES = 56 * 1024 * 1024

IN_ROWS = 1024
IN_COLS = 1024
OUT_ROWS = 256
DN_GROUP = 4
DN_ROWS = 8 * CHUNK


def _bdot(a, b):
    return jnp.dot(a.astype(BF16), b.astype(BF16), preferred_element_type=F32)


def _bdot_nt(a, b):
    return lax.dot_general(a.astype(BF16), b.astype(BF16), (((1,), (1,)), ((), ())),
                           preferred_element_type=F32)


def _bdot_tn(a, b):
    return lax.dot_general(a.astype(BF16), b.astype(BF16), (((0,), (0,)), ((), ())),
                           preferred_element_type=F32)


def _split2(x):
    hi = x.astype(BF16)
    lo = (x - hi.astype(F32)).astype(BF16)
    return hi, lo


def _sigmoid_pair(x):
    e = jnp.exp(-jnp.abs(x))
    r = 1.0 / (1.0 + e)
    er = e * r
    pos = x >= 0
    return jnp.where(pos, r, er), jnp.where(pos, er, r)


def _sigmoid(x):
    return 1.0 / (1.0 + jnp.exp(-x))


def _silu(x):
    return x * _sigmoid(x)


def _softplus(x):
    return jnp.maximum(x, 0.0) + jnp.log1p(jnp.exp(-jnp.abs(x)))


def _gated_head_norm(o, z, w):
    o = o * lax.rsqrt(jnp.mean(o * o, axis=-1, keepdims=True) + NORM_EPS) * w
    return o * _silu(z)


def _inproj_kernel(h_ref, nw_ref, w_ref, wbat_ref, proj_ref, bat_ref, hn_ref):
    contract_last = (((1,), (1,)), ((), ()))

    @pl.when(pl.program_id(1) == 0)
    def _():
        x = h_ref[...]
        ms = jnp.mean(x * x, axis=-1, keepdims=True)
        hn = (x * lax.rsqrt(ms + NORM_EPS) * nw_ref[...]).astype(BF16)
        hn_ref[...] = hn
        bat_ref[...] = lax.dot_general(wbat_ref[0], hn, contract_last, preferred_element_type=F32)

    proj_ref[...] = lax.dot_general(hn_ref[...], w_ref[0], contract_last, preferred_element_type=F32)


def _inproj(h, nw, w_t, layer, gate0, n_gate):
    tokens, d_model = h.shape
    width = w_t.shape[1] - n_gate
    assert gate0 % IN_COLS == 0 and width % IN_COLS == 0
    skip_from = gate0 // IN_COLS
    grid = (tokens // IN_ROWS, width // IN_COLS)
    return pl.pallas_call(
        _inproj_kernel,
        grid=grid,
        in_specs=[
            pl.BlockSpec((IN_ROWS, d_model), lambda i, j: (i, 0)),
            pl.BlockSpec((1, d_model), lambda i, j: (0, 0)),
            pl.BlockSpec((pl.Element(1), pl.Element(IN_COLS), pl.Element(d_model)),
                         lambda i, j: (layer, (j * (IN_COLS // n_gate) + (j >= skip_from).astype(jnp.int32))
                                       * n_gate, 0)),
            pl.BlockSpec((pl.Element(1), pl.Element(n_gate), pl.Element(d_model)),
                         lambda i, j: (layer, gate0, 0)),
        ],
        out_specs=[
            pl.BlockSpec((IN_ROWS, IN_COLS), lambda i, j: (i, j)),
            pl.BlockSpec((n_gate, IN_ROWS), lambda i, j: (0, i)),
        ],
        out_shape=[
            jax.ShapeDtypeStruct((tokens, width), F32),
            jax.ShapeDtypeStruct((n_gate, tokens), F32),
        ],
        scratch_shapes=[pltpu.VMEM((IN_ROWS, d_model), BF16)],
        compiler_params=pltpu.CompilerParams(
            dimension_semantics=("arbitrary", "arbitrary"),
            vmem_limit_bytes=VMEM_LIMIT_BYTES),
        name="inproj",
    )(h, nw, w_t, w_t)


def _level_constants():
    t = np.arange(CHUNK)
    sums = [(t[None, :] <= t[:, None])]
    masks = [np.eye(CHUNK, dtype=bool)]
    for lvl in range(1, N_LEVELS + 1):
        size, half = 1 << lvl, 1 << (lvl - 1)
        ref = (t // size) * size + half - 1
        sums.append(t[None, :] <= ref[:, None])
        same = (t[:, None] // size) == (t[None, :] // size)
        upper_r = ((t // half) % 2 == 1)[:, None]
        lower_s = ((t // half) % 2 == 0)[None, :]
        masks.append(same & upper_r & lower_s)
    sums.append(np.ones((CHUNK, CHUNK), dtype=bool))
    sums = np.concatenate(sums, axis=0).astype(np.float32)
    sums = np.concatenate([sums, sums], axis=1)
    masks = np.stack(masks, axis=0).astype(np.float32)
    assert masks.sum(axis=0).tolist() == np.tril(np.ones((CHUNK, CHUNK))).tolist()
    return sums, masks


def _pair_constants():
    k = np.arange(2 * PAIR)
    n2 = np.arange(2 * PAIR)
    n1 = np.arange(PAIR)
    k_chunk = (k % PAIR) // CHUNK
    k_pos = k % CHUNK
    col_sel = (k_chunk[:, None] == (n2 // PAIR)[None, :])
    row_cum = (k_chunk[:, None] == (n1 // CHUNK)[None, :]) & (k_pos[:, None] <= (n1 % CHUNK)[None, :])
    return col_sel.astype(np.float32), row_cum.astype(np.float32)


def _conv_silu(x_ref, w_ref, buf_ref, lanes):
    assert CONV_WIDTH == 4
    rows = x_ref.shape[0]
    x = x_ref[:, lanes]
    xe = jnp.concatenate([buf_ref[pl.ds(0, 8), lanes], x], axis=0)
    buf_ref[pl.ds(0, 8), lanes] = x[rows - 8:rows, :]
    w = w_ref[:, lanes]
    x1 = pltpu.roll(xe, 1, axis=0)
    near = w[3:4, :] * xe + w[2:3, :] * x1
    far = w[1:2, :] * xe + w[0:1, :] * x1
    acc = near + pltpu.roll(far, 2, axis=0)
    return _silu(acc[8:, :])


def _mixer_kernel(layer, alog_ref, dtb_ref, q_ref, k_ref, v_ref, z_ref, b_ref, a_ref,
                  wq_ref, wk_ref, wv_ref, nw_ref, masks_ref, colsel_ref, rowcum_ref,
                  hq_ref, hf_ref, hi_ref, hz_ref, lbl_ref, hnw_ref, sums_ref, hmasks_ref,
                  y_ref, yb_ref,
                  state_ref, qbuf_ref, kbuf_ref, vbuf_ref, hstate_ref):
    group = pl.program_id(1)
    width = q_ref.shape[1]

    @pl.when(pl.program_id(2) == 0)
    def _():
        hstate_ref[...] = jnp.zeros_like(hstate_ref)
        state_ref[...] = jnp.zeros_like(state_ref)
        qbuf_ref[pl.ds(0, 8), :] = jnp.zeros((8, width), F32)
        kbuf_ref[pl.ds(0, 8), :] = jnp.zeros((8, width), F32)
        vbuf_ref[pl.ds(0, 8), :] = jnp.zeros((8, width), F32)

    rows = q_ref.shape[0]
    n_pairs = rows // PAIR
    heads = range(DN_GROUP)
    units = [(j, p) for j in heads for p in range(n_pairs)]
    n_units = len(units)

    lane = lax.broadcasted_iota(jnp.int32, (CHUNK, PAIR), 1)
    row = lax.broadcasted_iota(jnp.int32, (CHUNK, PAIR), 0)
    col = lane & (CHUNK - 1)
    left = lane < CHUNK
    causal = col <= row
    strict = col < row
    eye = col == row
    left_b = jnp.where(left, 1.0, 0.0).astype(BF16)
    right_b = jnp.where(left, 0.0, 1.0).astype(BF16)
    level_b = [masks_ref[lvl].astype(BF16) for lvl in range(N_LEVELS + 1)]
    col_sel = colsel_ref[...]
    row_cum = rowcum_ref[...]

    def block_diag(m):
        return jnp.concatenate([m * left_b, m * right_b], axis=0)

    def pair_dot2(x_b, y_hi, y_lo):
        both = jnp.dot(x_b, jnp.concatenate([block_diag(y_hi), block_diag(y_lo)], axis=1),
                       preferred_element_type=F32)
        return both[:, 0:PAIR] + both[:, PAIR:2 * PAIR]

    q_h, k_h, v_h, beta_rows, g_rows = [], [], [], [], []
    for j in heads:
        sl = slice(j * HEAD_DIM, (j + 1) * HEAD_DIM)
        qj = _conv_silu(q_ref, wq_ref, qbuf_ref, sl)
        kj = _conv_silu(k_ref, wk_ref, kbuf_ref, sl)
        v_h.append(_conv_silu(v_ref, wv_ref, vbuf_ref, sl))
        q_h.append(qj * (lax.rsqrt(jnp.sum(qj * qj, axis=-1, keepdims=True) + L2_EPS) * (HEAD_DIM ** -0.5)))
        k_h.append(kj * lax.rsqrt(jnp.sum(kj * kj, axis=-1, keepdims=True) + L2_EPS))
        head = group * DN_GROUP + j
        beta_rows.append(_sigmoid(b_ref[j:j + 1, :]))
        rate = jnp.exp(jnp.full((1, rows), alog_ref[head], F32))
        g_rows.append(-rate * _softplus(a_ref[j:j + 1, :] + dtb_ref[head]))

    cols, g_rowcums = [], []
    for j, p in units:
        ps = slice(p * PAIR, (p + 1) * PAIR)
        g_b = jnp.broadcast_to(g_rows[j][:, ps], (CHUNK, PAIR))
        beta_b = jnp.broadcast_to(beta_rows[j][:, ps], (CHUNK, PAIR))
        lhs = jnp.concatenate([jnp.where(causal, g_b, 0.0), jnp.where(eye, beta_b, 0.0), g_b], axis=0)
        hi, lo = _split2(lhs)
        hilo = jnp.concatenate([hi, lo], axis=1)
        cols.append(jnp.dot(hilo, col_sel, preferred_element_type=F32))
        g_rowcums.append(jnp.dot(hilo[2 * CHUNK:3 * CHUNK], row_cum,
                                 preferred_element_type=F32))

    def stacked(u, block):
        c = cols[u][block * CHUNK:(block + 1) * CHUNK]
        return jnp.concatenate([c[:, 0:HEAD_DIM], c[:, HEAD_DIM:2 * HEAD_DIM]], axis=0)

    g_cols = [stacked(u, 0) for u in range(n_units)]
    beta_cols = [stacked(u, 1) for u in range(n_units)]
    g_lasts = [stacked(u, 2) for u in range(n_units)]
    exp_gs = [jnp.exp(g) for g in g_cols]
    q_u = [q_h[j][p * PAIR:(p + 1) * PAIR] for j, p in units]
    k_u = [k_h[j][p * PAIR:(p + 1) * PAIR] for j, p in units]
    v_u = [v_h[j][p * PAIR:(p + 1) * PAIR] for j, p in units]
    k_betas = [k_u[u] * beta_cols[u] for u in range(n_units)]

    prods = [_bdot_nt(jnp.concatenate([k_betas[u], q_u[u]], axis=0), k_u[u]) for u in range(n_units)]

    a_bs, qks, invs = [], [], []
    for u in range(n_units):
        g_pair = jnp.where(left, g_cols[u][0:CHUNK], g_cols[u][CHUNK:PAIR])
        diff = g_pair - g_rowcums[u]
        decay = jnp.where(causal, jnp.exp(jnp.where(causal, diff, 0.0)), 0.0)
        pr = prods[u]
        kk = jnp.where(left, pr[0:CHUNK], pr[CHUNK:PAIR])
        qk = jnp.where(left, pr[PAIR:PAIR + CHUNK], pr[PAIR + CHUNK:2 * PAIR])
        a_mat = jnp.where(strict, kk * decay, 0.0)
        a_bs.append(a_mat.astype(BF16))
        qks.append(qk * decay)
        invs.append(jnp.where(eye, 1.0, 0.0) - masks_ref[1] * a_mat)

    hgrn2 = _hgrn2_stream(layer, hq_ref, hf_ref, hi_ref, hz_ref, lbl_ref, hnw_ref, sums_ref, hmasks_ref,
                          yb_ref, hstate_ref)

    def advance(n):
        for _ in range(n * n_pairs // 2):
            next(hgrn2, None)

    for lvl in range(2, N_LEVELS + 1):
        t_split = [_split2(t) for t in invs]
        xs = [pair_dot2(a_bs[u] * level_b[lvl], *t_split[u]) for u in range(n_units)]
        advance(1)
        invs = [invs[u] - pair_dot2(t_split[u][0], *_split2(xs[u])) for u in range(n_units)]
        advance(1)

    sols, qk_sols, kt_sols = [], [], []
    for u in range(n_units):
        rhs = jnp.concatenate([v_u[u] * beta_cols[u], k_betas[u] * exp_gs[u]], axis=1)
        sols.append(jnp.dot(block_diag(invs[u].astype(BF16)), rhs.astype(BF16),
                            preferred_element_type=F32))
    advance(1)
    for u in range(n_units):
        qk_sols.append(jnp.dot(block_diag(qks[u].astype(BF16)), sols[u].astype(BF16),
                               preferred_element_type=F32))
    for u in range(n_units):
        k_tail = k_u[u] * jnp.exp(g_lasts[u] - g_cols[u])
        kt_sols.append([_bdot_tn(k_tail[c * CHUNK:(c + 1) * CHUNK], sols[u][c * CHUNK:(c + 1) * CHUNK])
                        for c in range(2)])
    advance(2)

    states = [state_ref[j] for j in heads]
    outs = [[] for _ in heads]
    for p in range(n_pairs):
        for c in range(2):
            cs = slice(c * CHUNK, (c + 1) * CHUNK)
            for j in heads:
                u = j * n_pairs + p
                kt_sol = kt_sols[u][c]
                lhs = jnp.concatenate(
                    [-kt_sol[:, HEAD_DIM:2 * HEAD_DIM],
                     q_u[u][cs] * exp_gs[u][cs] - qk_sols[u][cs, HEAD_DIM:2 * HEAD_DIM]], axis=0)
                both = _bdot(lhs, states[j])
                tail = jnp.exp(g_lasts[u][c * CHUNK:c * CHUNK + 1, :])
                states[j] = tail * states[j] + both[0:HEAD_DIM] + kt_sol[:, 0:HEAD_DIM]
                outs[j].append(both[HEAD_DIM:HEAD_DIM + CHUNK] + qk_sols[u][cs, 0:HEAD_DIM])
            next(hgrn2, None)
    for _ in hgrn2:
        pass

    for j in heads:
        state_ref[j] = states[j]
        sl = slice(j * HEAD_DIM, (j + 1) * HEAD_DIM)
        o = jnp.concatenate(outs[j], axis=0)
        y_ref[:, sl] = _gated_head_norm(o, z_ref[:, sl], nw_ref[...]).astype(y_ref.dtype)


def _hgrn2_stream(layer, q_ref, f_ref, i_ref, z_ref, lbl_ref, nw_ref, sums_ref, masks_ref, y_ref, state_ref):
    rows = q_ref.shape[0]
    n_chunks = rows // CHUNK
    heads = range(q_ref.shape[1] // HEAD_DIM)
    units = [(hd, c) for hd in heads for c in range(n_chunks)]

    logits = lbl_ref[...]
    ex = jnp.exp(logits - jnp.max(logits, axis=0, keepdims=True))
    probs = ex / jnp.sum(ex, axis=0, keepdims=True)
    lb_all = jnp.sum(probs[0:layer + 1], axis=0, keepdims=True) - probs[0:1]
    sums2 = sums_ref[...]

    per_head = {}

    def head_arrays(hd):
        if hd not in per_head:
            sl = slice(hd * HEAD_DIM, (hd + 1) * HEAD_DIM)
            lb = lb_all[:, sl]
            sig, nsig = _sigmoid_pair(f_ref[:, sl])
            per_head[hd] = dict(log_f=jnp.log(lb + (1.0 - lb) * sig), k=(1.0 - lb) * nsig,
                                q=_silu(q_ref[:, sl]), v=i_ref[:, sl])
        return per_head[hd]

    def cumulative(idx):
        hd, c = units[idx]
        hi, lo = _split2(head_arrays(hd)["log_f"][c * CHUNK:(c + 1) * CHUNK])
        return jnp.dot(sums2, jnp.concatenate([hi, lo], axis=0), preferred_element_type=F32)

    states = {hd: state_ref[hd] for hd in heads}
    outs = {hd: [] for hd in heads}

    def finish(hd, c, a_mat, inc, g, g_last):
        arrs = head_arrays(hd)
        cs = slice(c * CHUNK, (c + 1) * CHUNK)
        outs[hd].append(_bdot_nt(arrs["q"][cs] * jnp.exp(g), states[hd]) + _bdot(a_mat, arrs["v"][cs]))
        states[hd] = jnp.exp(g_last[0:1, :]) * states[hd] + inc
        if c == n_chunks - 1:
            sl = slice(hd * HEAD_DIM, (hd + 1) * HEAD_DIM)
            state_ref[hd] = states[hd]
            o = jnp.concatenate(outs[hd], axis=0)
            y_ref[:, sl] = _gated_head_norm(o, z_ref[:, sl], nw_ref[...]).astype(y_ref.dtype)

    LOOKAHEAD = 2
    g_alls = {idx: cumulative(idx) for idx in range(min(LOOKAHEAD, len(units)))}
    pending = None
    for idx, (hd, c) in enumerate(units):
        arrs = head_arrays(hd)
        cs = slice(c * CHUNK, (c + 1) * CHUNK)
        g_all = g_alls.pop(idx)
        g = g_all[0:CHUNK]
        g_last = g_all[(N_LEVELS + 1) * CHUNK:(N_LEVELS + 2) * CHUNK]
        q_b, k_b = arrs["q"][cs].astype(BF16), arrs["k"][cs].astype(BF16)
        a_mat = masks_ref[0] * _bdot_nt(q_b, k_b)
        for lvl in range(1, N_LEVELS + 1):
            wgt = jnp.exp(-jnp.abs(g - g_all[lvl * CHUNK:(lvl + 1) * CHUNK])).astype(BF16)
            a_mat = a_mat + masks_ref[lvl] * _bdot_nt(q_b * wgt, k_b * wgt)
        if idx + LOOKAHEAD < len(units):
            g_alls[idx + LOOKAHEAD] = cumulative(idx + LOOKAHEAD)
        inc = _bdot_tn(arrs["v"][cs], arrs["k"][cs] * jnp.exp(g_last - g))
        if pending is not None:
            finish(*pending)
        pending = (hd, c, a_mat, inc, g, g_last)
        yield
    finish(*pending)
    yield


def _mixer(proj, bat4, conv_w, a_log, dt_bias, dn_norm_w, masks_pair, col_sel, row_cum,
           lb_logits, hg_norm_w, sums, masks, layer, batch, seq, n_heads):
    tokens = proj.shape[0]
    depth = lb_logits.shape[0]
    steps = seq // DN_ROWS
    groups = n_heads // DN_GROUP
    width = DN_GROUP * HEAD_DIM
    row_map = lambda col0: (lambda b, g, t: (b * steps + t, col0 + g))
    gate_map = lambda kind: (lambda b, g, t: (kind, g, 0, b * steps + t))
    conv_map = lambda col0: (lambda b, g, t: (0, col0 + g))
    const2 = lambda b, g, t: (0, 0)
    const3 = lambda b, g, t: (0, 0, 0)
    smem = pl.BlockSpec(memory_space=pltpu.SMEM)
    tile = lambda k: pl.BlockSpec((DN_ROWS, width), row_map(k * groups))
    y_shape = jax.ShapeDtypeStruct((tokens, n_heads * HEAD_DIM), BF16)
    return pl.pallas_call(
        functools.partial(_mixer_kernel, layer),
        grid=(batch, groups, steps),
        in_specs=[
            smem, smem,
            tile(0), tile(1), tile(2), tile(3),
            pl.BlockSpec((None, None, DN_GROUP, DN_ROWS), gate_map(0)),
            pl.BlockSpec((None, None, DN_GROUP, DN_ROWS), gate_map(1)),
            pl.BlockSpec((CONV_WIDTH, width), conv_map(0)),
            pl.BlockSpec((CONV_WIDTH, width), conv_map(groups)),
            pl.BlockSpec((CONV_WIDTH, width), conv_map(2 * groups)),
            pl.BlockSpec((1, HEAD_DIM), const2),
            pl.BlockSpec(masks_pair.shape, const3),
            pl.BlockSpec(col_sel.shape, const2),
            pl.BlockSpec(row_cum.shape, const2),
            tile(4), tile(5), tile(6), tile(7),
            pl.BlockSpec((depth, width), lambda b, g, t: (0, g)),
            pl.BlockSpec((1, HEAD_DIM), const2),
            pl.BlockSpec(sums.shape, const2),
            pl.BlockSpec(masks.shape, const3),
        ],
        out_specs=[pl.BlockSpec((DN_ROWS, width), row_map(0)), pl.BlockSpec((DN_ROWS, width), row_map(0))],
        out_shape=[y_shape, y_shape],
        scratch_shapes=[
            pltpu.VMEM((DN_GROUP, HEAD_DIM, HEAD_DIM), F32),
            pltpu.VMEM((8, width), F32),
            pltpu.VMEM((8, width), F32),
            pltpu.VMEM((8, width), F32),
            pltpu.VMEM((DN_GROUP, HEAD_DIM, HEAD_DIM), F32),
        ],
        compiler_params=pltpu.CompilerParams(
            dimension_semantics=("arbitrary", "arbitrary", "arbitrary"),
            vmem_limit_bytes=VMEM_LIMIT_BYTES),
        name="mixer",
    )(a_log, dt_bias, proj, proj, proj, proj, bat4, bat4, conv_w, conv_w, conv_w, dn_norm_w,
      masks_pair, col_sel, row_cum, proj, proj, proj, proj, lb_logits, hg_norm_w, sums, masks)


def _outproj_kernel(final, h_ref, ya_ref, yb_ref, p_ref, wo_ref, wg_ref, wu_ref, nw_ref, o_ref):
    y = jnp.concatenate([ya_ref[...], yb_ref[...]], axis=1)
    h1 = h_ref[...] + jnp.dot(y, wo_ref[...], preferred_element_type=F32)
    gate = _sigmoid(jnp.dot(h1.astype(BF16), wg_ref[...], preferred_element_type=F32))
    up = jnp.dot(p_ref[...].astype(BF16), wu_ref[...], preferred_element_type=F32)
    h2 = h1 + up * gate
    if final:
        ms = jnp.mean(h2 * h2, axis=-1, keepdims=True)
        h2 = h2 * lax.rsqrt(ms + NORM_EPS) * nw_ref[...]
    o_ref[...] = h2


def _outproj(h, ya, yb, p, layer, w_out, w_gate, w_up, final_nw, final):
    tokens, d_model = h.shape
    half = ya.shape[1]
    ple = p.shape[2]
    resident = lambda w: pl.BlockSpec((None,) + w.shape[1:], lambda i: (layer, 0, 0),
                                      pipeline_mode=pl.Buffered(1))
    return pl.pallas_call(
        functools.partial(_outproj_kernel, final),
        grid=(tokens // OUT_ROWS,),
        in_specs=[
            pl.BlockSpec((OUT_ROWS, d_model), lambda i: (i, 0)),
            pl.BlockSpec((OUT_ROWS, half), lambda i: (i, 0)),
            pl.BlockSpec((OUT_ROWS, half), lambda i: (i, 0)),
            pl.BlockSpec((None, OUT_ROWS, ple), lambda i: (layer, i, 0)),
            resident(w_out),
            resident(w_gate),
            resident(w_up),
            pl.BlockSpec(final_nw.shape, lambda i: (0, 0)),
        ],
        out_specs=pl.BlockSpec((OUT_ROWS, d_model), lambda i: (i, 0)),
        out_shape=jax.ShapeDtypeStruct((tokens, d_model), F32),
        compiler_params=pltpu.CompilerParams(
            dimension_semantics=("arbitrary",),
            vmem_limit_bytes=VMEM_LIMIT_BYTES),
        name="outproj",
    )(h, ya, yb, p, w_out, w_gate, w_up, final_nw)


def kernel(x, p, norm_w, w_in, dn_conv_w, dn_A_log, dn_dt_bias, dn_norm_w, hg_lb_logits, hg_norm_w,
           w_out, w_ple_up, w_ple_gate, final_norm_w):
    batch, seq, d_model = x.shape
    depth = w_in.shape[0]
    dn_heads = dn_A_log.shape[1]
    dn_width = dn_heads * HEAD_DIM
    tokens = batch * seq
    gate0 = 4 * dn_width
    gate1 = gate0 + 2 * dn_heads
    assert hg_lb_logits.shape[1] == dn_width, "the mixer steps both head groups in lockstep"
    assert dn_heads % DN_GROUP == 0 and seq % DN_ROWS == 0

    sums_np, masks_np = _level_constants()
    col_sel_np, row_cum_np = _pair_constants()
    sums = jnp.asarray(sums_np, dtype=BF16)
    masks = jnp.asarray(masks_np)
    masks_pair = jnp.asarray(np.concatenate([masks_np, masks_np], axis=2))
    col_sel = jnp.asarray(col_sel_np, dtype=BF16)
    row_cum = jnp.asarray(row_cum_np, dtype=BF16)

    w_in_t = jnp.swapaxes(w_in, 1, 2).astype(BF16)
    w_out_b, w_gate_b, w_up_b = w_out.astype(BF16), w_ple_gate.astype(BF16), w_ple_up.astype(BF16)

    h = x.reshape(tokens, d_model)
    for l in range(depth):
        proj, bat = _inproj(h, norm_w[l][None, :], w_in_t, l, gate0, gate1 - gate0)
        bat4 = bat.reshape(2, dn_heads // DN_GROUP, DN_GROUP, tokens)
        ya, yb = _mixer(proj, bat4, dn_conv_w[l], dn_A_log[l], dn_dt_bias[l], dn_norm_w[l][None, :],
                        masks_pair, col_sel, row_cum, hg_lb_logits, hg_norm_w[l][None, :], sums, masks, l,
                        batch, seq, dn_heads)
        h = _outproj(h, ya, yb, p.reshape(depth, tokens, -1), l, w_out_b, w_gate_b, w_up_b,
                     final_norm_w[None, :], l == depth - 1)
    return h.reshape(batch, seq, d_model)
```

```python
import functools

import jax
import jax.numpy as jnp
import numpy as np
from jax import lax
from jax.experimental import pallas as pl
from jax.experimental.pallas import tpu as pltpu

F32 = jnp.float32
BF16 = jnp.bfloat16

HEAD_DIM = 128
CHUNK = 64
PAIR = 2 * CHUNK
CONV_WIDTH = 4
NORM_EPS = 1e-6
L2_EPS = 1e-6
N_LEVELS = 6
VMEM_LIMIT_BYTES = 56 * 1024 * 1024

IN_ROWS = 1024
IN_COLS = 1024
OUT_ROWS = 256
DN_GROUP = 4
DN_ROWS = 8 * CHUNK


def _bdot(a, b):
    return jnp.dot(a.astype(BF16), b.astype(BF16), preferred_element_type=F32)


def _bdot_nt(a, b):
    return lax.dot_general(a.astype(BF16), b.astype(BF16), (((1,), (1,)), ((), ())),
                           preferred_element_type=F32)


def _bdot_tn(a, b):
    return lax.dot_general(a.astype(BF16), b.astype(BF16), (((0,), (0,)), ((), ())),
                           preferred_element_type=F32)


def _split2(x):
    hi = x.astype(BF16)
    lo = (x - hi.astype(F32)).astype(BF16)
    return hi, lo


def _sigmoid_pair(x):
    e = jnp.exp(-jnp.abs(x))
    r = 1.0 / (1.0 + e)
    er = e * r
    pos = x >= 0
    return jnp.where(pos, r, er), jnp.where(pos, er, r)


def _sigmoid(x):
    return jax.nn.sigmoid(x)


def _silu(x):
    return x * _sigmoid(x)


def _softplus(x):
    return jnp.maximum(x, 0.0) + jnp.log1p(jnp.exp(-jnp.abs(x)))


def _gated_head_norm(o, z, w):
    o = o * lax.rsqrt(jnp.mean(o * o, axis=-1, keepdims=True) + NORM_EPS) * w
    return o * _silu(z)


def _inproj_kernel(h_ref, nw_ref, w_ref, wbat_ref, proj_ref, bat_ref, hn_ref):
    contract_last = (((1,), (1,)), ((), ()))

    @pl.when(pl.program_id(1) == 0)
    def _():
        x = h_ref[...]
        ms = jnp.mean(x * x, axis=-1, keepdims=True)
        hn = (x * lax.rsqrt(ms + NORM_EPS) * nw_ref[...]).astype(BF16)
        hn_ref[...] = hn
        bat_ref[...] = lax.dot_general(wbat_ref[0], hn, contract_last, preferred_element_type=F32)

    proj_ref[...] = lax.dot_general(hn_ref[...], w_ref[0], contract_last, preferred_element_type=F32)


def _inproj(h, nw, w_t, layer, gate0, n_gate):
    tokens, d_model = h.shape
    width = w_t.shape[1] - n_gate
    assert gate0 % IN_COLS == 0 and width % IN_COLS == 0
    skip_from = gate0 // IN_COLS
    grid = (tokens // IN_ROWS, width // IN_COLS)
    return pl.pallas_call(
        _inproj_kernel,
        grid=grid,
        in_specs=[
            pl.BlockSpec((IN_ROWS, d_model), lambda i, j: (i, 0)),
            pl.BlockSpec((1, d_model), lambda i, j: (0, 0)),
            pl.BlockSpec((pl.Element(1), pl.Element(IN_COLS), pl.Element(d_model)),
                         lambda i, j: (layer, (j * (IN_COLS // n_gate) + (j >= skip_from).astype(jnp.int32))
                                       * n_gate, 0)),
            pl.BlockSpec((pl.Element(1), pl.Element(n_gate), pl.Element(d_model)),
                         lambda i, j: (layer, gate0, 0)),
        ],
        out_specs=[
            pl.BlockSpec((IN_ROWS, IN_COLS), lambda i, j: (i, j)),
            pl.BlockSpec((n_gate, IN_ROWS), lambda i, j: (0, i)),
        ],
        out_shape=[
            jax.ShapeDtypeStruct((tokens, width), F32),
            jax.ShapeDtypeStruct((n_gate, tokens), F32),
        ],
        scratch_shapes=[pltpu.VMEM((IN_ROWS, d_model), BF16)],
        compiler_params=pltpu.CompilerParams(
            dimension_semantics=("arbitrary", "arbitrary"),
            vmem_limit_bytes=VMEM_LIMIT_BYTES),
        name="inproj",
    )(h, nw, w_t, w_t)


def _level_constants():
    t = np.arange(CHUNK)
    sums = [(t[None, :] <= t[:, None])]
    masks = [np.eye(CHUNK, dtype=bool)]
    for lvl in range(1, N_LEVELS + 1):
        size, half = 1 << lvl, 1 << (lvl - 1)
        ref = (t // size) * size + half - 1
        sums.append(t[None, :] <= ref[:, None])
        same = (t[:, None] // size) == (t[None, :] // size)
        upper_r = ((t // half) % 2 == 1)[:, None]
        lower_s = ((t // half) % 2 == 0)[None, :]
        masks.append(same & upper_r & lower_s)
    sums.append(np.ones((CHUNK, CHUNK), dtype=bool))
    sums = np.concatenate(sums, axis=0).astype(np.float32)
    sums = np.concatenate([sums, sums], axis=1)
    masks = np.stack(masks, axis=0).astype(np.float32)
    assert masks.sum(axis=0).tolist() == np.tril(np.ones((CHUNK, CHUNK))).tolist()
    return sums, masks


def _pair_constants():
    k = np.arange(2 * PAIR)
    n2 = np.arange(2 * PAIR)
    n1 = np.arange(PAIR)
    k_chunk = (k % PAIR) // CHUNK
    k_pos = k % CHUNK
    col_sel = (k_chunk[:, None] == (n2 // PAIR)[None, :])
    row_cum = (k_chunk[:, None] == (n1 // CHUNK)[None, :]) & (k_pos[:, None] <= (n1 % CHUNK)[None, :])
    return col_sel.astype(np.float32), row_cum.astype(np.float32)


def _conv_silu(x_ref, w_ref, buf_ref, lanes):
    assert CONV_WIDTH == 4
    rows = x_ref.shape[0]
    x = x_ref[:, lanes]
    xe = jnp.concatenate([buf_ref[pl.ds(0, 8), lanes], x], axis=0)
    buf_ref[pl.ds(0, 8), lanes] = x[rows - 8:rows, :]
    w = w_ref[:, lanes]
    x1 = pltpu.roll(xe, 1, axis=0)
    near = w[3:4, :] * xe + w[2:3, :] * x1
    far = w[1:2, :] * xe + w[0:1, :] * x1
    acc = near + pltpu.roll(far, 2, axis=0)
    return _silu(acc[8:, :])


def _mixer_kernel(layer, alog_ref, dtb_ref, q_ref, k_ref, v_ref, z_ref, b_ref, a_ref,
                  wq_ref, wk_ref, wv_ref, nw_ref, masks_ref, colsel_ref, rowcum_ref,
                  hq_ref, hf_ref, hi_ref, hz_ref, lbl_ref, hnw_ref, sums_ref, hmasks_ref,
                  y_ref, yb_ref,
                  state_ref, qbuf_ref, kbuf_ref, vbuf_ref, hstate_ref):
    group = pl.program_id(1)
    width = q_ref.shape[1]

    @pl.when(pl.program_id(2) == 0)
    def _():
        hstate_ref[...] = jnp.zeros_like(hstate_ref)
        state_ref[...] = jnp.zeros_like(state_ref)
        qbuf_ref[pl.ds(0, 8), :] = jnp.zeros((8, width), F32)
        kbuf_ref[pl.ds(0, 8), :] = jnp.zeros((8, width), F32)
        vbuf_ref[pl.ds(0, 8), :] = jnp.zeros((8, width), F32)

    rows = q_ref.shape[0]
    n_pairs = rows // PAIR
    heads = range(DN_GROUP)
    units = [(j, p) for j in heads for p in range(n_pairs)]
    n_units = len(units)

    lane = lax.broadcasted_iota(jnp.int32, (CHUNK, PAIR), 1)
    row = lax.broadcasted_iota(jnp.int32, (CHUNK, PAIR), 0)
    col = lane & (CHUNK - 1)
    left = lane < CHUNK
    causal = col <= row
    strict = col < row
    eye = col == row
    left_b = jnp.where(left, 1.0, 0.0).astype(BF16)
    right_b = jnp.where(left, 0.0, 1.0).astype(BF16)
    level_b = [masks_ref[lvl].astype(BF16) for lvl in range(N_LEVELS + 1)]
    col_sel = colsel_ref[...]
    row_cum = rowcum_ref[...]

    def block_diag(m):
        return jnp.concatenate([m * left_b, m * right_b], axis=0)

    def pair_dot2(x_b, y_hi, y_lo):
        both = jnp.dot(x_b, jnp.concatenate([block_diag(y_hi), block_diag(y_lo)], axis=1),
                       preferred_element_type=F32)
        return both[:, 0:PAIR] + both[:, PAIR:2 * PAIR]

    q_h, k_h, v_h, beta_rows, g_rows = [], [], [], [], []
    for j in heads:
        sl = slice(j * HEAD_DIM, (j + 1) * HEAD_DIM)
        qj = _conv_silu(q_ref, wq_ref, qbuf_ref, sl)
        kj = _conv_silu(k_ref, wk_ref, kbuf_ref, sl)
        v_h.append(_conv_silu(v_ref, wv_ref, vbuf_ref, sl))
        q_h.append(qj * (lax.rsqrt(jnp.sum(qj * qj, axis=-1, keepdims=True) + L2_EPS) * (HEAD_DIM ** -0.5)))
        k_h.append(kj * lax.rsqrt(jnp.sum(kj * kj, axis=-1, keepdims=True) + L2_EPS))
        head = group * DN_GROUP + j
        beta_rows.append(_sigmoid(b_ref[j:j + 1, :]))
        rate = jnp.exp(jnp.full((1, rows), alog_ref[head], F32))
        g_rows.append(-rate * _softplus(a_ref[j:j + 1, :] + dtb_ref[head]))

    cols, g_rowcums = [], []
    for j, p in units:
        ps = slice(p * PAIR, (p + 1) * PAIR)
        g_b = jnp.broadcast_to(g_rows[j][:, ps], (CHUNK, PAIR))
        beta_b = jnp.broadcast_to(beta_rows[j][:, ps], (CHUNK, PAIR))
        lhs = jnp.concatenate([jnp.where(causal, g_b, 0.0), jnp.where(eye, beta_b, 0.0), g_b], axis=0)
        hi, lo = _split2(lhs)
        hilo = jnp.concatenate([hi, lo], axis=1)
        cols.append(jnp.dot(hilo, col_sel, preferred_element_type=F32))
        g_rowcums.append(jnp.dot(hilo[2 * CHUNK:3 * CHUNK], row_cum,
                                 preferred_element_type=F32))

    def stacked(u, block):
        c = cols[u][block * CHUNK:(block + 1) * CHUNK]
        return jnp.concatenate([c[:, 0:HEAD_DIM], c[:, HEAD_DIM:2 * HEAD_DIM]], axis=0)

    g_cols = [stacked(u, 0) for u in range(n_units)]
    beta_cols = [stacked(u, 1) for u in range(n_units)]
    g_lasts = [stacked(u, 2) for u in range(n_units)]
    exp_gs = [jnp.exp(g) for g in g_cols]
    q_u = [q_h[j][p * PAIR:(p + 1) * PAIR] for j, p in units]
    k_u = [k_h[j][p * PAIR:(p + 1) * PAIR] for j, p in units]
    v_u = [v_h[j][p * PAIR:(p + 1) * PAIR] for j, p in units]
    k_betas = [k_u[u] * beta_cols[u] for u in range(n_units)]

    prods = [_bdot_nt(jnp.concatenate([k_betas[u], q_u[u]], axis=0), k_u[u]) for u in range(n_units)]

    a_bs, qks, invs = [], [], []
    for u in range(n_units):
        g_pair = jnp.where(left, g_cols[u][0:CHUNK], g_cols[u][CHUNK:PAIR])
        diff = g_pair - g_rowcums[u]
        decay = jnp.where(causal, jnp.exp(jnp.where(causal, diff, 0.0)), 0.0)
        pr = prods[u]
        kk = jnp.where(left, pr[0:CHUNK], pr[CHUNK:PAIR])
        qk = jnp.where(left, pr[PAIR:PAIR + CHUNK], pr[PAIR + CHUNK:2 * PAIR])
        a_mat = jnp.where(strict, kk * decay, 0.0)
        a_bs.append(a_mat.astype(BF16))
        qks.append(qk * decay)
        invs.append(jnp.where(eye, 1.0, 0.0) - masks_ref[1] * a_mat)

    hgrn2 = _hgrn2_stream(layer, hq_ref, hf_ref, hi_ref, hz_ref, lbl_ref, hnw_ref, sums_ref, hmasks_ref,
                          yb_ref, hstate_ref)

    def advance(n):
        for _ in range(n * n_pairs // 2):
            next(hgrn2, None)

    for lvl in range(2, N_LEVELS + 1):
        t_split = [_split2(t) for t in invs]
        xs = [pair_dot2(a_bs[u] * level_b[lvl], *t_split[u]) for u in range(n_units)]
        advance(1)
        invs = [invs[u] - pair_dot2(t_split[u][0], *_split2(xs[u])) for u in range(n_units)]
        advance(1)

    sols, qk_sols, kt_sols = [], [], []
    for u in range(n_units):
        rhs = jnp.concatenate([v_u[u] * beta_cols[u], k_betas[u] * exp_gs[u]], axis=1)
        sols.append(jnp.dot(block_diag(invs[u].astype(BF16)), rhs.astype(BF16),
                            preferred_element_type=F32))
    advance(1)
    for u in range(n_units):
        qk_sols.append(jnp.dot(block_diag(qks[u].astype(BF16)), sols[u].astype(BF16),
                               preferred_element_type=F32))
    for u in range(n_units):
        k_tail = k_u[u] * jnp.exp(g_lasts[u] - g_cols[u])
        kt_sols.append([_bdot_tn(k_tail[c * CHUNK:(c + 1) * CHUNK], sols[u][c * CHUNK:(c + 1) * CHUNK])
                        for c in range(2)])
    advance(2)

    states = [state_ref[j] for j in heads]
    outs = [[] for _ in heads]
    for p in range(n_pairs):
        for c in range(2):
            cs = slice(c * CHUNK, (c + 1) * CHUNK)
            for j in heads:
                u = j * n_pairs + p
                kt_sol = kt_sols[u][c]
                lhs = jnp.concatenate(
                    [-kt_sol[:, HEAD_DIM:2 * HEAD_DIM],
                     q_u[u][cs] * exp_gs[u][cs] - qk_sols[u][cs, HEAD_DIM:2 * HEAD_DIM]], axis=0)
                both = _bdot(lhs, states[j])
                tail = jnp.exp(g_lasts[u][c * CHUNK:c * CHUNK + 1, :])
                states[j] = tail * states[j] + both[0:HEAD_DIM] + kt_sol[:, 0:HEAD_DIM]
                outs[j].append(both[HEAD_DIM:HEAD_DIM + CHUNK] + qk_sols[u][cs, 0:HEAD_DIM])
            next(hgrn2, None)
    for _ in hgrn2:
        pass

    for j in heads:
        state_ref[j] = states[j]
        sl = slice(j * HEAD_DIM, (j + 1) * HEAD_DIM)
        o = jnp.concatenate(outs[j], axis=0)
        y_ref[:, sl] = _gated_head_norm(o, z_ref[:, sl], nw_ref[...]).astype(y_ref.dtype)


def _hgrn2_stream(layer, q_ref, f_ref, i_ref, z_ref, lbl_ref, nw_ref, sums_ref, masks_ref, y_ref, state_ref):
    rows = q_ref.shape[0]
    n_chunks = rows // CHUNK
    heads = range(q_ref.shape[1] // HEAD_DIM)
    units = [(hd, c) for hd in heads for c in range(n_chunks)]

    logits = lbl_ref[...]
    ex = jnp.exp(logits - jnp.max(logits, axis=0, keepdims=True))
    probs = ex / jnp.sum(ex, axis=0, keepdims=True)
    lb_all = jnp.sum(probs[0:layer + 1], axis=0, keepdims=True) - probs[0:1]
    sums2 = sums_ref[...]

    per_head = {}

    def head_arrays(hd):
        if hd not in per_head:
            sl = slice(hd * HEAD_DIM, (hd + 1) * HEAD_DIM)
            lb = lb_all[:, sl]
            sig, nsig = _sigmoid_pair(f_ref[:, sl])
            per_head[hd] = dict(log_f=jnp.log(lb + (1.0 - lb) * sig), k=(1.0 - lb) * nsig,
                                q=_silu(q_ref[:, sl]), v=i_ref[:, sl])
        return per_head[hd]

    def cumulative(idx):
        hd, c = units[idx]
        hi, lo = _split2(head_arrays(hd)["log_f"][c * CHUNK:(c + 1) * CHUNK])
        return jnp.dot(sums2, jnp.concatenate([hi, lo], axis=0), preferred_element_type=F32)

    states = {hd: state_ref[hd] for hd in heads}
    outs = {hd: [] for hd in heads}

    def finish(hd, c, a_mat, inc, g, g_last):
        arrs = head_arrays(hd)
        cs = slice(c * CHUNK, (c + 1) * CHUNK)
        outs[hd].append(_bdot_nt(arrs["q"][cs] * jnp.exp(g), states[hd]) + _bdot(a_mat, arrs["v"][cs]))
        states[hd] = jnp.exp(g_last[0:1, :]) * states[hd] + inc
        if c == n_chunks - 1:
            sl = slice(hd * HEAD_DIM, (hd + 1) * HEAD_DIM)
            state_ref[hd] = states[hd]
            o = jnp.concatenate(outs[hd], axis=0)
            y_ref[:, sl] = _gated_head_norm(o, z_ref[:, sl], nw_ref[...]).astype(y_ref.dtype)

    LOOKAHEAD = 2
    g_alls = {idx: cumulative(idx) for idx in range(min(LOOKAHEAD, len(units)))}
    pending = None
    for idx, (hd, c) in enumerate(units):
        arrs = head_arrays(hd)
        cs = slice(c * CHUNK, (c + 1) * CHUNK)
        g_all = g_alls.pop(idx)
        g = g_all[0:CHUNK]
        g_last = g_all[(N_LEVELS + 1) * CHUNK:(N_LEVELS + 2) * CHUNK]
        q_b, k_b = arrs["q"][cs].astype(BF16), arrs["k"][cs].astype(BF16)
        a_mat = masks_ref[0] * _bdot_nt(q_b, k_b)
        for lvl in range(1, N_LEVELS + 1):
            wgt = jnp.exp(-jnp.abs(g - g_all[lvl * CHUNK:(lvl + 1) * CHUNK])).astype(BF16)
            a_mat = a_mat + masks_ref[lvl] * _bdot_nt(q_b * wgt, k_b * wgt)
        if idx + LOOKAHEAD < len(units):
            g_alls[idx + LOOKAHEAD] = cumulative(idx + LOOKAHEAD)
        inc = _bdot_tn(arrs["v"][cs], arrs["k"][cs] * jnp.exp(g_last - g))
        if pending is not None:
            finish(*pending)
        pending = (hd, c, a_mat, inc, g, g_last)
        yield
    finish(*pending)
    yield


def _mixer(proj, bat4, conv_w, a_log, dt_bias, dn_norm_w, masks_pair, col_sel, row_cum,
           lb_logits, hg_norm_w, sums, masks, layer, batch, seq, n_heads):
    tokens = proj.shape[0]
    depth = lb_logits.shape[0]
    steps = seq // DN_ROWS
    groups = n_heads // DN_GROUP
    width = DN_GROUP * HEAD_DIM
    row_map = lambda col0: (lambda b, g, t: (b * steps + t, col0 + g))
    gate_map = lambda kind: (lambda b, g, t: (kind, g, 0, b * steps + t))
    conv_map = lambda col0: (lambda b, g, t: (0, col0 + g))
    const2 = lambda b, g, t: (0, 0)
    const3 = lambda b, g, t: (0, 0, 0)
    smem = pl.BlockSpec(memory_space=pltpu.SMEM)
    tile = lambda k: pl.BlockSpec((DN_ROWS, width), row_map(k * groups))
    y_shape = jax.ShapeDtypeStruct((tokens, n_heads * HEAD_DIM), BF16)
    return pl.pallas_call(
        functools.partial(_mixer_kernel, layer),
        grid=(batch, groups, steps),
        in_specs=[
            smem, smem,
            tile(0), tile(1), tile(2), tile(3),
            pl.BlockSpec((None, None, DN_GROUP, DN_ROWS), gate_map(0)),
            pl.BlockSpec((None, None, DN_GROUP, DN_ROWS), gate_map(1)),
            pl.BlockSpec((CONV_WIDTH, width), conv_map(0)),
            pl.BlockSpec((CONV_WIDTH, width), conv_map(groups)),
            pl.BlockSpec((CONV_WIDTH, width), conv_map(2 * groups)),
            pl.BlockSpec((1, HEAD_DIM), const2),
            pl.BlockSpec(masks_pair.shape, const3),
            pl.BlockSpec(col_sel.shape, const2),
            pl.BlockSpec(row_cum.shape, const2),
            tile(4), tile(5), tile(6), tile(7),
            pl.BlockSpec((depth, width), lambda b, g, t: (0, g)),
            pl.BlockSpec((1, HEAD_DIM), const2),
            pl.BlockSpec(sums.shape, const2),
            pl.BlockSpec(masks.shape, const3),
        ],
        out_specs=[pl.BlockSpec((DN_ROWS, width), row_map(0)), pl.BlockSpec((DN_ROWS, width), row_map(0))],
        out_shape=[y_shape, y_shape],
        scratch_shapes=[
            pltpu.VMEM((DN_GROUP, HEAD_DIM, HEAD_DIM), F32),
            pltpu.VMEM((8, width), F32),
            pltpu.VMEM((8, width), F32),
            pltpu.VMEM((8, width), F32),
            pltpu.VMEM((DN_GROUP, HEAD_DIM, HEAD_DIM), F32),
        ],
        compiler_params=pltpu.CompilerParams(
            dimension_semantics=("arbitrary", "arbitrary", "arbitrary"),
            vmem_limit_bytes=VMEM_LIMIT_BYTES),
        name="mixer",
    )(a_log, dt_bias, proj, proj, proj, proj, bat4, bat4, conv_w, conv_w, conv_w, dn_norm_w,
      masks_pair, col_sel, row_cum, proj, proj, proj, proj, lb_logits, hg_norm_w, sums, masks)


def _outproj_kernel(final, h_ref, ya_ref, yb_ref, p_ref, wo_ref, wg_ref, wu_ref, nw_ref, o_ref):
    y = jnp.concatenate([ya_ref[...], yb_ref[...]], axis=1)
    h1 = h_ref[...] + jnp.dot(y, wo_ref[...], preferred_element_type=F32)
    gate = _sigmoid(jnp.dot(h1.astype(BF16), wg_ref[...], preferred_element_type=F32))
    up = jnp.dot(p_ref[...].astype(BF16), wu_ref[...], preferred_element_type=F32)
    h2 = h1 + up * gate
    if final:
        ms = jnp.mean(h2 * h2, axis=-1, keepdims=True)
        h2 = h2 * lax.rsqrt(ms + NORM_EPS) * nw_ref[...]
    o_ref[...] = h2


def _outproj(h, ya, yb, p, layer, w_out, w_gate, w_up, final_nw, final):
    tokens, d_model = h.shape
    half = ya.shape[1]
    ple = p.shape[2]
    resident = lambda w: pl.BlockSpec((None,) + w.shape[1:], lambda i: (layer, 0, 0),
                                      pipeline_mode=pl.Buffered(1))
    return pl.pallas_call(
        functools.partial(_outproj_kernel, final),
        grid=(tokens // OUT_ROWS,),
        in_specs=[
            pl.BlockSpec((OUT_ROWS, d_model), lambda i: (i, 0)),
            pl.BlockSpec((OUT_ROWS, half), lambda i: (i, 0)),
            pl.BlockSpec((OUT_ROWS, half), lambda i: (i, 0)),
            pl.BlockSpec((None, OUT_ROWS, ple), lambda i: (layer, i, 0)),
            resident(w_out),
            resident(w_gate),
            resident(w_up),
            pl.BlockSpec(final_nw.shape, lambda i: (0, 0)),
        ],
        out_specs=pl.BlockSpec((OUT_ROWS, d_model), lambda i: (i, 0)),
        out_shape=jax.ShapeDtypeStruct((tokens, d_model), F32),
        compiler_params=pltpu.CompilerParams(
            dimension_semantics=("arbitrary",),
            vmem_limit_bytes=VMEM_LIMIT_BYTES),
        name="outproj",
    )(h, ya, yb, p, w_out, w_gate, w_up, final_nw)


def kernel(x, p, norm_w, w_in, dn_conv_w, dn_A_log, dn_dt_bias, dn_norm_w, hg_lb_logits, hg_norm_w,
           w_out, w_ple_up, w_ple_gate, final_norm_w):
    batch, seq, d_model = x.shape
    depth = w_in.shape[0]
    dn_heads = dn_A_log.shape[1]
    dn_width = dn_heads * HEAD_DIM
    tokens = batch * seq
    gate0 = 4 * dn_width
    gate1 = gate0 + 2 * dn_heads
    assert hg_lb_logits.shape[1] == dn_width, "the mixer steps both head groups in lockstep"
    assert dn_heads % DN_GROUP == 0 and seq % DN_ROWS == 0

    sums_np, masks_np = _level_constants()
    col_sel_np, row_cum_np = _pair_constants()
    sums = jnp.asarray(sums_np, dtype=BF16)
    masks = jnp.asarray(masks_np)
    masks_pair = jnp.asarray(np.concatenate([masks_np, masks_np], axis=2))
    col_sel = jnp.asarray(col_sel_np, dtype=BF16)
    row_cum = jnp.asarray(row_cum_np, dtype=BF16)

    w_in_t = jnp.swapaxes(w_in, 1, 2).astype(BF16)
    w_out_b, w_gate_b, w_up_b = w_out.astype(BF16), w_ple_gate.astype(BF16), w_ple_up.astype(BF16)

    h = x.reshape(tokens, d_model)
    for l in range(depth):
        proj, bat = _inproj(h, norm_w[l][None, :], w_in_t, l, gate0, gate1 - gate0)
        bat4 = bat.reshape(2, dn_heads // DN_GROUP, DN_GROUP, tokens)
        ya, yb = _mixer(proj, bat4, dn_conv_w[l], dn_A_log[l], dn_dt_bias[l], dn_norm_w[l][None, :],
                        masks_pair, col_sel, row_cum, hg_lb_logits, hg_norm_w[l][None, :], sums, masks, l,
                        batch, seq, dn_heads)
        h = _outproj(h, ya, yb, p.reshape(depth, tokens, -1), l, w_out_b, w_gate_b, w_up_b,
                     final_norm_w[None, :], l == depth - 1)
    return h.reshape(batch, seq, d_model)
```

```python
import functools

import jax
import jax.numpy as jnp
import numpy as np
from jax import lax
from jax.experimental import pallas as pl
from jax.experimental.pallas import tpu as pltpu

F32 = jnp.float32
BF16 = jnp.bfloat16

HEAD_DIM = 128
CHUNK = 64
PAIR = 2 * CHUNK
CONV_WIDTH = 4
NORM_EPS = 1e-6
L2_EPS = 1e-6
N_LEVELS = 6
VMEM_LIMIT_BYTES = 56 * 1024 * 1024

IN_ROWS = 1024
IN_COLS = 1024
OUT_ROWS = 256
DN_GROUP = 4
DN_ROWS = 8 * CHUNK


def _bdot(a, b):
    return jnp.dot(a.astype(BF16), b.astype(BF16), preferred_element_type=F32)


def _bdot_nt(a, b):
    return lax.dot_general(a.astype(BF16), b.astype(BF16), (((1,), (1,)), ((), ())),
                           preferred_element_type=F32)


def _bdot_tn(a, b):
    return lax.dot_general(a.astype(BF16), b.astype(BF16), (((0,), (0,)), ((), ())),
                           preferred_element_type=F32)


def _split2(x):
    hi = x.astype(BF16)
    lo = (x - hi.astype(F32)).astype(BF16)
    return hi, lo


def _sigmoid_pair(x):
    return jax.nn.sigmoid(x), jax.nn.sigmoid(-x)


def _sigmoid(x):
    return jax.nn.sigmoid(x)


def _silu(x):
    return x * _sigmoid(x)


def _softplus(x):
    return jnp.maximum(x, 0.0) + jnp.log1p(jnp.exp(-jnp.abs(x)))


def _gated_head_norm(o, z, w):
    o = o * lax.rsqrt(jnp.mean(o * o, axis=-1, keepdims=True) + NORM_EPS) * w
    return o * _silu(z)


def _inproj_kernel(h_ref, nw_ref, w_ref, wbat_ref, proj_ref, bat_ref, hn_ref):
    contract_last = (((1,), (1,)), ((), ()))

    @pl.when(pl.program_id(1) == 0)
    def _():
        x = h_ref[...]
        ms = jnp.mean(x * x, axis=-1, keepdims=True)
        hn = (x * lax.rsqrt(ms + NORM_EPS) * nw_ref[...]).astype(BF16)
        hn_ref[...] = hn
        bat_ref[...] = lax.dot_general(wbat_ref[0], hn, contract_last, preferred_element_type=F32)

    proj_ref[...] = lax.dot_general(hn_ref[...], w_ref[0], contract_last, preferred_element_type=F32)


def _inproj(h, nw, w_t, layer, gate0, n_gate):
    tokens, d_model = h.shape
    width = w_t.shape[1] - n_gate
    assert gate0 % IN_COLS == 0 and width % IN_COLS == 0
    skip_from = gate0 // IN_COLS
    grid = (tokens // IN_ROWS, width // IN_COLS)
    return pl.pallas_call(
        _inproj_kernel,
        grid=grid,
        in_specs=[
            pl.BlockSpec((IN_ROWS, d_model), lambda i, j: (i, 0)),
            pl.BlockSpec((1, d_model), lambda i, j: (0, 0)),
            pl.BlockSpec((pl.Element(1), pl.Element(IN_COLS), pl.Element(d_model)),
                         lambda i, j: (layer, (j * (IN_COLS // n_gate) + (j >= skip_from).astype(jnp.int32))
                                       * n_gate, 0)),
            pl.BlockSpec((pl.Element(1), pl.Element(n_gate), pl.Element(d_model)),
                         lambda i, j: (layer, gate0, 0)),
        ],
        out_specs=[
            pl.BlockSpec((IN_ROWS, IN_COLS), lambda i, j: (i, j)),
            pl.BlockSpec((n_gate, IN_ROWS), lambda i, j: (0, i)),
        ],
        out_shape=[
            jax.ShapeDtypeStruct((tokens, width), F32),
            jax.ShapeDtypeStruct((n_gate, tokens), F32),
        ],
        scratch_shapes=[pltpu.VMEM((IN_ROWS, d_model), BF16)],
        compiler_params=pltpu.CompilerParams(
            dimension_semantics=("arbitrary", "arbitrary"),
            vmem_limit_bytes=VMEM_LIMIT_BYTES),
        name="inproj",
    )(h, nw, w_t, w_t)


def _level_constants():
    t = np.arange(CHUNK)
    sums = [(t[None, :] <= t[:, None])]
    masks = [np.eye(CHUNK, dtype=bool)]
    for lvl in range(1, N_LEVELS + 1):
        size, half = 1 << lvl, 1 << (lvl - 1)
        ref = (t // size) * size + half - 1
        sums.append(t[None, :] <= ref[:, None])
        same = (t[:, None] // size) == (t[None, :] // size)
        upper_r = ((t // half) % 2 == 1)[:, None]
        lower_s = ((t // half) % 2 == 0)[None, :]
        masks.append(same & upper_r & lower_s)
    sums.append(np.ones((CHUNK, CHUNK), dtype=bool))
    sums = np.concatenate(sums, axis=0).astype(np.float32)
    sums = np.concatenate([sums, sums], axis=1)
    masks = np.stack(masks, axis=0).astype(np.float32)
    assert masks.sum(axis=0).tolist() == np.tril(np.ones((CHUNK, CHUNK))).tolist()
    return sums, masks


def _pair_constants():
    k = np.arange(2 * PAIR)
    n2 = np.arange(2 * PAIR)
    n1 = np.arange(PAIR)
    k_chunk = (k % PAIR) // CHUNK
    k_pos = k % CHUNK
    col_sel = (k_chunk[:, None] == (n2 // PAIR)[None, :])
    row_cum = (k_chunk[:, None] == (n1 // CHUNK)[None, :]) & (k_pos[:, None] <= (n1 % CHUNK)[None, :])
    return col_sel.astype(np.float32), row_cum.astype(np.float32)


def _conv_silu(x_ref, w_ref, buf_ref, lanes):
    assert CONV_WIDTH == 4
    rows = x_ref.shape[0]
    x = x_ref[:, lanes]
    xe = jnp.concatenate([buf_ref[pl.ds(0, 8), lanes], x], axis=0)
    buf_ref[pl.ds(0, 8), lanes] = x[rows - 8:rows, :]
    w = w_ref[:, lanes]
    x1 = pltpu.roll(xe, 1, axis=0)
    near = w[3:4, :] * xe + w[2:3, :] * x1
    far = w[1:2, :] * xe + w[0:1, :] * x1
    acc = near + pltpu.roll(far, 2, axis=0)
    return _silu(acc[8:, :])


def _mixer_kernel(layer, alog_ref, dtb_ref, q_ref, k_ref, v_ref, z_ref, b_ref, a_ref,
                  wq_ref, wk_ref, wv_ref, nw_ref, masks_ref, colsel_ref, rowcum_ref,
                  hq_ref, hf_ref, hi_ref, hz_ref, lbl_ref, hnw_ref, sums_ref, hmasks_ref,
                  y_ref, yb_ref,
                  state_ref, qbuf_ref, kbuf_ref, vbuf_ref, hstate_ref):
    group = pl.program_id(1)
    width = q_ref.shape[1]

    @pl.when(pl.program_id(2) == 0)
    def _():
        hstate_ref[...] = jnp.zeros_like(hstate_ref)
        state_ref[...] = jnp.zeros_like(state_ref)
        qbuf_ref[pl.ds(0, 8), :] = jnp.zeros((8, width), F32)
        kbuf_ref[pl.ds(0, 8), :] = jnp.zeros((8, width), F32)
        vbuf_ref[pl.ds(0, 8), :] = jnp.zeros((8, width), F32)

    rows = q_ref.shape[0]
    n_pairs = rows // PAIR
    heads = range(DN_GROUP)
    units = [(j, p) for j in heads for p in range(n_pairs)]
    n_units = len(units)

    lane = lax.broadcasted_iota(jnp.int32, (CHUNK, PAIR), 1)
    row = lax.broadcasted_iota(jnp.int32, (CHUNK, PAIR), 0)
    col = lane & (CHUNK - 1)
    left = lane < CHUNK
    causal = col <= row
    strict = col < row
    eye = col == row
    left_b = jnp.where(left, 1.0, 0.0).astype(BF16)
    right_b = jnp.where(left, 0.0, 1.0).astype(BF16)
    level_b = [masks_ref[lvl].astype(BF16) for lvl in range(N_LEVELS + 1)]
    col_sel = colsel_ref[...]
    row_cum = rowcum_ref[...]

    def block_diag(m):
        return jnp.concatenate([m * left_b, m * right_b], axis=0)

    def pair_dot2(x_b, y_hi, y_lo):
        both = jnp.dot(x_b, jnp.concatenate([block_diag(y_hi), block_diag(y_lo)], axis=1),
                       preferred_element_type=F32)
        return both[:, 0:PAIR] + both[:, PAIR:2 * PAIR]

    q_h, k_h, v_h, beta_rows, g_rows = [], [], [], [], []
    for j in heads:
        sl = slice(j * HEAD_DIM, (j + 1) * HEAD_DIM)
        qj = _conv_silu(q_ref, wq_ref, qbuf_ref, sl)
        kj = _conv_silu(k_ref, wk_ref, kbuf_ref, sl)
        v_h.append(_conv_silu(v_ref, wv_ref, vbuf_ref, sl))
        q_h.append(qj * (lax.rsqrt(jnp.sum(qj * qj, axis=-1, keepdims=True) + L2_EPS) * (HEAD_DIM ** -0.5)))
        k_h.append(kj * lax.rsqrt(jnp.sum(kj * kj, axis=-1, keepdims=True) + L2_EPS))
        head = group * DN_GROUP + j
        beta_rows.append(_sigmoid(b_ref[j:j + 1, :]))
        rate = jnp.exp(jnp.full((1, rows), alog_ref[head], F32))
        g_rows.append(-rate * _softplus(a_ref[j:j + 1, :] + dtb_ref[head]))

    cols, g_rowcums = [], []
    for j, p in units:
        ps = slice(p * PAIR, (p + 1) * PAIR)
        g_b = jnp.broadcast_to(g_rows[j][:, ps], (CHUNK, PAIR))
        beta_b = jnp.broadcast_to(beta_rows[j][:, ps], (CHUNK, PAIR))
        lhs = jnp.concatenate([jnp.where(causal, g_b, 0.0), jnp.where(eye, beta_b, 0.0), g_b], axis=0)
        hi, lo = _split2(lhs)
        hilo = jnp.concatenate([hi, lo], axis=1)
        cols.append(jnp.dot(hilo, col_sel, preferred_element_type=F32))
        g_rowcums.append(jnp.dot(hilo[2 * CHUNK:3 * CHUNK], row_cum,
                                 preferred_element_type=F32))

    def stacked(u, block):
        c = cols[u][block * CHUNK:(block + 1) * CHUNK]
        return jnp.concatenate([c[:, 0:HEAD_DIM], c[:, HEAD_DIM:2 * HEAD_DIM]], axis=0)

    g_cols = [stacked(u, 0) for u in range(n_units)]
    beta_cols = [stacked(u, 1) for u in range(n_units)]
    g_lasts = [stacked(u, 2) for u in range(n_units)]
    exp_gs = [jnp.exp(g) for g in g_cols]
    q_u = [q_h[j][p * PAIR:(p + 1) * PAIR] for j, p in units]
    k_u = [k_h[j][p * PAIR:(p + 1) * PAIR] for j, p in units]
    v_u = [v_h[j][p * PAIR:(p + 1) * PAIR] for j, p in units]
    k_betas = [k_u[u] * beta_cols[u] for u in range(n_units)]

    prods = [_bdot_nt(jnp.concatenate([k_betas[u], q_u[u]], axis=0), k_u[u]) for u in range(n_units)]

    a_bs, qks, invs = [], [], []
    for u in range(n_units):
        g_pair = jnp.where(left, g_cols[u][0:CHUNK], g_cols[u][CHUNK:PAIR])
        diff = g_pair - g_rowcums[u]
        decay = jnp.where(causal, jnp.exp(jnp.where(causal, diff, 0.0)), 0.0)
        pr = prods[u]
        kk = jnp.where(left, pr[0:CHUNK], pr[CHUNK:PAIR])
        qk = jnp.where(left, pr[PAIR:PAIR + CHUNK], pr[PAIR + CHUNK:2 * PAIR])
        a_mat = jnp.where(strict, kk * decay, 0.0)
        a_bs.append(a_mat.astype(BF16))
        qks.append(qk * decay)
        invs.append(jnp.where(eye, 1.0, 0.0) - masks_ref[1] * a_mat)

    hgrn2 = _hgrn2_stream(layer, hq_ref, hf_ref, hi_ref, hz_ref, lbl_ref, hnw_ref, sums_ref, hmasks_ref,
                          yb_ref, hstate_ref)

    def advance(n):
        for _ in range(n * n_pairs // 2):
            next(hgrn2, None)

    for lvl in range(2, N_LEVELS + 1):
        t_split = [_split2(t) for t in invs]
        xs = [pair_dot2(a_bs[u] * level_b[lvl], *t_split[u]) for u in range(n_units)]
        advance(1)
        invs = [invs[u] - pair_dot2(t_split[u][0], *_split2(xs[u])) for u in range(n_units)]
        advance(1)

    sols, qk_sols, kt_sols = [], [], []
    for u in range(n_units):
        rhs = jnp.concatenate([v_u[u] * beta_cols[u], k_betas[u] * exp_gs[u]], axis=1)
        sols.append(jnp.dot(block_diag(invs[u].astype(BF16)), rhs.astype(BF16),
                            preferred_element_type=F32))
    advance(1)
    for u in range(n_units):
        qk_sols.append(jnp.dot(block_diag(qks[u].astype(BF16)), sols[u].astype(BF16),
                               preferred_element_type=F32))
    for u in range(n_units):
        k_tail = k_u[u] * jnp.exp(g_lasts[u] - g_cols[u])
        kt_sols.append([_bdot_tn(k_tail[c * CHUNK:(c + 1) * CHUNK], sols[u][c * CHUNK:(c + 1) * CHUNK])
                        for c in range(2)])
    advance(2)

    states = [state_ref[j] for j in heads]
    outs = [[] for _ in heads]
    for p in range(n_pairs):
        for c in range(2):
            cs = slice(c * CHUNK, (c + 1) * CHUNK)
            for j in heads:
                u = j * n_pairs + p
                kt_sol = kt_sols[u][c]
                lhs = jnp.concatenate(
                    [-kt_sol[:, HEAD_DIM:2 * HEAD_DIM],
                     q_u[u][cs] * exp_gs[u][cs] - qk_sols[u][cs, HEAD_DIM:2 * HEAD_DIM]], axis=0)
                both = _bdot(lhs, states[j])
                tail = jnp.exp(g_lasts[u][c * CHUNK:c * CHUNK + 1, :])
                states[j] = tail * states[j] + both[0:HEAD_DIM] + kt_sol[:, 0:HEAD_DIM]
                outs[j].append(both[HEAD_DIM:HEAD_DIM + CHUNK] + qk_sols[u][cs, 0:HEAD_DIM])
            next(hgrn2, None)
    for _ in hgrn2:
        pass

    for j in heads:
        state_ref[j] = states[j]
        sl = slice(j * HEAD_DIM, (j + 1) * HEAD_DIM)
        o = jnp.concatenate(outs[j], axis=0)
        y_ref[:, sl] = _gated_head_norm(o, z_ref[:, sl], nw_ref[...]).astype(y_ref.dtype)


def _hgrn2_stream(layer, q_ref, f_ref, i_ref, z_ref, lbl_ref, nw_ref, sums_ref, masks_ref, y_ref, state_ref):
    rows = q_ref.shape[0]
    n_chunks = rows // CHUNK
    heads = range(q_ref.shape[1] // HEAD_DIM)
    units = [(hd, c) for hd in heads for c in range(n_chunks)]

    logits = lbl_ref[...]
    ex = jnp.exp(logits - jnp.max(logits, axis=0, keepdims=True))
    probs = ex / jnp.sum(ex, axis=0, keepdims=True)
    lb_all = jnp.sum(probs[0:layer + 1], axis=0, keepdims=True) - probs[0:1]
    sums2 = sums_ref[...]

    per_head = {}

    def head_arrays(hd):
        if hd not in per_head:
            sl = slice(hd * HEAD_DIM, (hd + 1) * HEAD_DIM)
            lb = lb_all[:, sl]
            sig, nsig = _sigmoid_pair(f_ref[:, sl])
            per_head[hd] = dict(log_f=jnp.log(lb + (1.0 - lb) * sig), k=(1.0 - lb) * nsig,
                                q=_silu(q_ref[:, sl]), v=i_ref[:, sl])
        return per_head[hd]

    def cumulative(idx):
        hd, c = units[idx]
        hi, lo = _split2(head_arrays(hd)["log_f"][c * CHUNK:(c + 1) * CHUNK])
        return jnp.dot(sums2, jnp.concatenate([hi, lo], axis=0), preferred_element_type=F32)

    states = {hd: state_ref[hd] for hd in heads}
    outs = {hd: [] for hd in heads}

    def finish(hd, c, a_mat, inc, g, g_last):
        arrs = head_arrays(hd)
        cs = slice(c * CHUNK, (c + 1) * CHUNK)
        outs[hd].append(_bdot_nt(arrs["q"][cs] * jnp.exp(g), states[hd]) + _bdot(a_mat, arrs["v"][cs]))
        states[hd] = jnp.exp(g_last[0:1, :]) * states[hd] + inc
        if c == n_chunks - 1:
            sl = slice(hd * HEAD_DIM, (hd + 1) * HEAD_DIM)
            state_ref[hd] = states[hd]
            o = jnp.concatenate(outs[hd], axis=0)
            y_ref[:, sl] = _gated_head_norm(o, z_ref[:, sl], nw_ref[...]).astype(y_ref.dtype)

    LOOKAHEAD = 2
    g_alls = {idx: cumulative(idx) for idx in range(min(LOOKAHEAD, len(units)))}
    pending = None
    for idx, (hd, c) in enumerate(units):
        arrs = head_arrays(hd)
        cs = slice(c * CHUNK, (c + 1) * CHUNK)
        g_all = g_alls.pop(idx)
        g = g_all[0:CHUNK]
        g_last = g_all[(N_LEVELS + 1) * CHUNK:(N_LEVELS + 2) * CHUNK]
        q_b, k_b = arrs["q"][cs].astype(BF16), arrs["k"][cs].astype(BF16)
        a_mat = masks_ref[0] * _bdot_nt(q_b, k_b)
        for lvl in range(1, N_LEVELS + 1):
            wgt = jnp.exp(-jnp.abs(g - g_all[lvl * CHUNK:(lvl + 1) * CHUNK])).astype(BF16)
            a_mat = a_mat + masks_ref[lvl] * _bdot_nt(q_b * wgt, k_b * wgt)
        if idx + LOOKAHEAD < len(units):
            g_alls[idx + LOOKAHEAD] = cumulative(idx + LOOKAHEAD)
        inc = _bdot_tn(arrs["v"][cs], arrs["k"][cs] * jnp.exp(g_last - g))
        if pending is not None:
            finish(*pending)
        pending = (hd, c, a_mat, inc, g, g_last)
        yield
    finish(*pending)
    yield


def _mixer(proj, bat4, conv_w, a_log, dt_bias, dn_norm_w, masks_pair, col_sel, row_cum,
           lb_logits, hg_norm_w, sums, masks, layer, batch, seq, n_heads):
    tokens = proj.shape[0]
    depth = lb_logits.shape[0]
    steps = seq // DN_ROWS
    groups = n_heads // DN_GROUP
    width = DN_GROUP * HEAD_DIM
    row_map = lambda col0: (lambda b, g, t: (b * steps + t, col0 + g))
    gate_map = lambda kind: (lambda b, g, t: (kind, g, 0, b * steps + t))
    conv_map = lambda col0: (lambda b, g, t: (0, col0 + g))
    const2 = lambda b, g, t: (0, 0)
    const3 = lambda b, g, t: (0, 0, 0)
    smem = pl.BlockSpec(memory_space=pltpu.SMEM)
    tile = lambda k: pl.BlockSpec((DN_ROWS, width), row_map(k * groups))
    y_shape = jax.ShapeDtypeStruct((tokens, n_heads * HEAD_DIM), BF16)
    return pl.pallas_call(
        functools.partial(_mixer_kernel, layer),
        grid=(batch, groups, steps),
        in_specs=[
            smem, smem,
            tile(0), tile(1), tile(2), tile(3),
            pl.BlockSpec((None, None, DN_GROUP, DN_ROWS), gate_map(0)),
            pl.BlockSpec((None, None, DN_GROUP, DN_ROWS), gate_map(1)),
            pl.BlockSpec((CONV_WIDTH, width), conv_map(0)),
            pl.BlockSpec((CONV_WIDTH, width), conv_map(groups)),
            pl.BlockSpec((CONV_WIDTH, width), conv_map(2 * groups)),
            pl.BlockSpec((1, HEAD_DIM), const2),
            pl.BlockSpec(masks_pair.shape, const3),
            pl.BlockSpec(col_sel.shape, const2),
            pl.BlockSpec(row_cum.shape, const2),
            tile(4), tile(5), tile(6), tile(7),
            pl.BlockSpec((depth, width), lambda b, g, t: (0, g)),
            pl.BlockSpec((1, HEAD_DIM), const2),
            pl.BlockSpec(sums.shape, const2),
            pl.BlockSpec(masks.shape, const3),
        ],
        out_specs=[pl.BlockSpec((DN_ROWS, width), row_map(0)), pl.BlockSpec((DN_ROWS, width), row_map(0))],
        out_shape=[y_shape, y_shape],
        scratch_shapes=[
            pltpu.VMEM((DN_GROUP, HEAD_DIM, HEAD_DIM), F32),
            pltpu.VMEM((8, width), F32),
            pltpu.VMEM((8, width), F32),
            pltpu.VMEM((8, width), F32),
            pltpu.VMEM((DN_GROUP, HEAD_DIM, HEAD_DIM), F32),
        ],
        compiler_params=pltpu.CompilerParams(
            dimension_semantics=("arbitrary", "arbitrary", "arbitrary"),
            vmem_limit_bytes=VMEM_LIMIT_BYTES),
        name="mixer",
    )(a_log, dt_bias, proj, proj, proj, proj, bat4, bat4, conv_w, conv_w, conv_w, dn_norm_w,
      masks_pair, col_sel, row_cum, proj, proj, proj, proj, lb_logits, hg_norm_w, sums, masks)


def _outproj_kernel(final, h_ref, ya_ref, yb_ref, p_ref, wo_ref, wg_ref, wu_ref, nw_ref, o_ref):
    y = jnp.concatenate([ya_ref[...], yb_ref[...]], axis=1)
    h1 = h_ref[...] + jnp.dot(y, wo_ref[...], preferred_element_type=F32)
    gate = _sigmoid(jnp.dot(h1.astype(BF16), wg_ref[...], preferred_element_type=F32))
    up = jnp.dot(p_ref[...].astype(BF16), wu_ref[...], preferred_element_type=F32)
    h2 = h1 + up * gate
    if final:
        ms = jnp.mean(h2 * h2, axis=-1, keepdims=True)
        h2 = h2 * lax.rsqrt(ms + NORM_EPS) * nw_ref[...]
    o_ref[...] = h2


def _outproj(h, ya, yb, p, layer, w_out, w_gate, w_up, final_nw, final):
    tokens, d_model = h.shape
    half = ya.shape[1]
    ple = p.shape[2]
    resident = lambda w: pl.BlockSpec((None,) + w.shape[1:], lambda i: (layer, 0, 0),
                                      pipeline_mode=pl.Buffered(1))
    return pl.pallas_call(
        functools.partial(_outproj_kernel, final),
        grid=(tokens // OUT_ROWS,),
        in_specs=[
            pl.BlockSpec((OUT_ROWS, d_model), lambda i: (i, 0)),
            pl.BlockSpec((OUT_ROWS, half), lambda i: (i, 0)),
            pl.BlockSpec((OUT_ROWS, half), lambda i: (i, 0)),
            pl.BlockSpec((None, OUT_ROWS, ple), lambda i: (layer, i, 0)),
            resident(w_out),
            resident(w_gate),
            resident(w_up),
            pl.BlockSpec(final_nw.shape, lambda i: (0, 0)),
        ],
        out_specs=pl.BlockSpec((OUT_ROWS, d_model), lambda i: (i, 0)),
        out_shape=jax.ShapeDtypeStruct((tokens, d_model), F32),
        compiler_params=pltpu.CompilerParams(
            dimension_semantics=("arbitrary",),
            vmem_limit_bytes=VMEM_LIMIT_BYTES),
        name="outproj",
    )(h, ya, yb, p, w_out, w_gate, w_up, final_nw)


def kernel(x, p, norm_w, w_in, dn_conv_w, dn_A_log, dn_dt_bias, dn_norm_w, hg_lb_logits, hg_norm_w,
           w_out, w_ple_up, w_ple_gate, final_norm_w):
    batch, seq, d_model = x.shape
    depth = w_in.shape[0]
    dn_heads = dn_A_log.shape[1]
    dn_width = dn_heads * HEAD_DIM
    tokens = batch * seq
    gate0 = 4 * dn_width
    gate1 = gate0 + 2 * dn_heads
    assert hg_lb_logits.shape[1] == dn_width, "the mixer steps both head groups in lockstep"
    assert dn_heads % DN_GROUP == 0 and seq % DN_ROWS == 0

    sums_np, masks_np = _level_constants()
    col_sel_np, row_cum_np = _pair_constants()
    sums = jnp.asarray(sums_np, dtype=BF16)
    masks = jnp.asarray(masks_np)
    masks_pair = jnp.asarray(np.concatenate([masks_np, masks_np], axis=2))
    col_sel = jnp.asarray(col_sel_np, dtype=BF16)
    row_cum = jnp.asarray(row_cum_np, dtype=BF16)

    w_in_t = jnp.swapaxes(w_in, 1, 2).astype(BF16)
    w_out_b, w_gate_b, w_up_b = w_out.astype(BF16), w_ple_gate.astype(BF16), w_ple_up.astype(BF16)

    h = x.reshape(tokens, d_model)
    for l in range(depth):
        proj, bat = _inproj(h, norm_w[l][None, :], w_in_t, l, gate0, gate1 - gate0)
        bat4 = bat.reshape(2, dn_heads // DN_GROUP, DN_GROUP, tokens)
        ya, yb = _mixer(proj, bat4, dn_conv_w[l], dn_A_log[l], dn_dt_bias[l], dn_norm_w[l][None, :],
                        masks_pair, col_sel, row_cum, hg_lb_logits, hg_norm_w[l][None, :], sums, masks, l,
                        batch, seq, dn_heads)
        h = _outproj(h, ya, yb, p.reshape(depth, tokens, -1), l, w_out_b, w_gate_b, w_up_b,
                     final_norm_w[None, :], l == depth - 1)
    return h.reshape(batch, seq, d_model)
```
